```python
import math
import jax
import jax.numpy as jnp
from jax import lax
import numpy as np

D_MODEL = 1024
BATCH = 8
SEQ = 4096
DEPTH = 1
DEC_BATCH = 128
DEC_SEQ = 4
PAST_LEN = 8192
PAGE_SIZE = 128

HEAD_DIM = 64
SB_HEADS = 8
MOBA_HEADS = 8
N_KV_HEADS = SB_HEADS + MOBA_HEADS
SB_WIDTH = SB_HEADS * HEAD_DIM
MOBA_WIDTH = MOBA_HEADS * HEAD_DIM
IN_COLS = 3 * SB_WIDTH + 3 * MOBA_WIDTH + 2 * D_MODEL
ROPE_THETA = 10000.0
SB_Q_BLOCK = 128
MOBA_BLOCK = 256
MOBA_TOPK = 3
MOBA_Q_BLOCK = 32
N_EXPERTS = 256
EXPERT_TOPK = 8
N_EXPERT_GROUPS = 8
TOPK_GROUPS = 4
ROUTED_SCALE = 2.5
D_EXPERT = 256
MOE_ROW_BLOCK = 128
PLE_DIM = 256
LN_EPS = 1e-5
DEEPNORM_ALPHA = (2 * DEPTH) ** 0.25
DEEPNORM_BETA = (8 * DEPTH) ** -0.25

kernel_name = 'hybrid_stickbreak_moba_moe_step'

F32 = jnp.float32


def layer_norm(x, g, b):
    xf = x.astype(F32)
    mu = jnp.mean(xf, axis=-1, keepdims=True)
    xc = xf - mu
    var = jnp.mean(xc * xc, axis=-1, keepdims=True)
    return (xc * lax.rsqrt(var + LN_EPS) * g.astype(F32) + b.astype(F32)).astype(x.dtype)


def rope(x, pos):
    half = HEAD_DIM // 2
    inv_freq = jnp.power(ROPE_THETA, -jnp.arange(half, dtype=F32) / half)
    ang = pos.astype(F32)[:, None] * inv_freq[None, :]
    cos = jnp.cos(ang)[None, :, None, :]
    sin = jnp.sin(ang)[None, :, None, :]
    x1 = x[..., :half].astype(F32)
    x2 = x[..., half:].astype(F32)
    return jnp.concatenate([x1 * cos - x2 * sin, x2 * cos + x1 * sin], axis=-1).astype(x.dtype)


def stick_breaking_attention(q, k, v, start):
    b, t, h, d = q.shape
    n_keys = k.shape[1]
    qb = min(SB_Q_BLOCK, t)
    nq = t // qb
    k_pos = jnp.arange(n_keys)
    scale = 1.0 / math.sqrt(d)
    q_blocks = jnp.moveaxis(q.reshape(b, nq, qb, h, d), 1, 0)

    def one_block(args):
        qi, bi = args
        q_pos = start + bi * qb + jnp.arange(qb)
        z = jnp.einsum('bqhd,bkhd->bhqk', qi, k, preferred_element_type=F32) * scale
        strict = k_pos[None, :] < q_pos[:, None]
        log_keep = jnp.where(strict, jax.nn.log_sigmoid(-z), 0.0)
        log_after = lax.cumsum(log_keep, axis=3, reverse=True) - log_keep
        w = jnp.where(strict, jnp.exp(jax.nn.log_sigmoid(z) + log_after), 0.0)
        return jnp.einsum('bhqk,bkhd->bqhd', w.astype(v.dtype), v)

    out = lax.map(one_block, (q_blocks, jnp.arange(nq)))
    return jnp.moveaxis(out, 0, 1).reshape(b, t, h * d)


def moba_attention(q, k, v, start):
    b, t, h, d = q.shape
    n_keys = k.shape[1]
    nb = -(-n_keys // MOBA_BLOCK)
    pad = nb * MOBA_BLOCK - n_keys
    kb = jnp.pad(k, ((0, 0), (0, pad), (0, 0), (0, 0))).reshape(b, nb, MOBA_BLOCK, h, d)
    vb = jnp.pad(v, ((0, 0), (0, pad), (0, 0), (0, 0))).reshape(b, nb, MOBA_BLOCK, h, d)
    k_mean = jnp.mean(kb, axis=2, dtype=F32)
    n_sel = min(MOBA_TOPK, nb)
    qb = min(MOBA_Q_BLOCK, t)
    nq = t // qb
    scale = 1.0 / math.sqrt(d)
    b_ar = jnp.arange(b)[:, None, None, None]
    h_ar = jnp.arange(h)[None, :, None, None]
    blk_ar = jnp.arange(nb)
    in_blk = jnp.arange(MOBA_BLOCK)
    q_blocks = jnp.moveaxis(q.reshape(b, nq, qb, h, d), 1, 0)

    def one_block(args):
        qi, bi = args
        q_pos = start + bi * qb + jnp.arange(qb)
        own = (start + bi * qb) // MOBA_BLOCK
        gate = jnp.einsum('bqhd,bnhd->bhqn', qi.astype(F32), k_mean)
        gate = jnp.where(blk_ar < own, gate, -jnp.inf)
        _, idx = lax.top_k(gate, n_sel)
        sel_ok = idx < own
        k_sel = kb[b_ar, idx, :, h_ar, :]
        v_sel = vb[b_ar, idx, :, h_ar, :]
        s_sel = jnp.einsum('bqhd,bhqnkd->bhqnk', qi, k_sel, preferred_element_type=F32) * scale
        s_sel = jnp.where(sel_ok[..., None], s_sel, -jnp.inf).reshape(b, h, qb, n_sel * MOBA_BLOCK)
        k_own = lax.dynamic_index_in_dim(kb, own, axis=1, keepdims=False)
        v_own = lax.dynamic_index_in_dim(vb, own, axis=1, keepdims=False)
        own_pos = own * MOBA_BLOCK + in_blk
        s_own = jnp.einsum('bqhd,bkhd->bhqk', qi, k_own, preferred_element_type=F32) * scale
        s_own = jnp.where(own_pos[None, :] <= q_pos[:, None], s_own, -jnp.inf)
        p = jax.nn.softmax(jnp.concatenate([s_sel, s_own], axis=-1), axis=-1)
        p_sel = p[..., : n_sel * MOBA_BLOCK].reshape(b, h, qb, n_sel, MOBA_BLOCK).astype(v.dtype)
        p_own = p[..., n_sel * MOBA_BLOCK:].astype(v.dtype)
        return (jnp.einsum('bhqnk,bhqnkd->bqhd', p_sel, v_sel)
                + jnp.einsum('bhqk,bkhd->bqhd', p_own, v_own))

    out = lax.map(one_block, (q_blocks, jnp.arange(nq)))
    return jnp.moveaxis(out, 0, 1).reshape(b, t, h * d)


def moe_ffn(x, w_router, b_router, w_exp_gu, w_exp_down, w_sh_gu, w_sh_down):
    n, d = x.shape
    scores = jax.nn.sigmoid(jnp.dot(x, w_router, preferred_element_type=F32))
    biased = scores + b_router.astype(F32)
    per_group = N_EXPERTS // N_EXPERT_GROUPS
    group_score = lax.top_k(biased.reshape(n, N_EXPERT_GROUPS, per_group), 2)[0].sum(-1)
    _, top_groups = lax.top_k(group_score, TOPK_GROUPS)
    group_keep = jnp.zeros((n, N_EXPERT_GROUPS), bool).at[jnp.arange(n)[:, None], top_groups].set(True)
    masked = jnp.where(jnp.repeat(group_keep, per_group, axis=1), biased, -jnp.inf)
    _, top_e = lax.top_k(masked, EXPERT_TOPK)
    gate = jnp.take_along_axis(scores, top_e, axis=1)
    gate = gate / jnp.sum(gate, axis=-1, keepdims=True) * ROUTED_SCALE

    n_assign = n * EXPERT_TOPK
    flat_e = top_e.reshape(-1)
    order = jnp.argsort(flat_e)
    sorted_e = flat_e[order]
    sorted_tok = (order // EXPERT_TOPK).astype(jnp.int32)
    sorted_gate = gate.reshape(-1)[order]
    counts = jnp.bincount(flat_e, length=N_EXPERTS)
    padded = (counts + MOE_ROW_BLOCK - 1) // MOE_ROW_BLOCK * MOE_ROW_BLOCK
    pad_end = jnp.cumsum(padded)
    pad_start = pad_end - padded
    raw_start = jnp.cumsum(counts) - counts
    dest = pad_start[sorted_e] + jnp.arange(n_assign) - raw_start[sorted_e]
    n_blocks = -(-(n_assign + N_EXPERTS * (MOE_ROW_BLOCK - 1)) // MOE_ROW_BLOCK)
    rows = n_blocks * MOE_ROW_BLOCK
    row_tok = jnp.full((rows,), n, jnp.int32).at[dest].set(sorted_tok)
    row_gate = jnp.zeros((rows,), F32).at[dest].set(sorted_gate)
    block_expert = jnp.minimum(
        jnp.searchsorted(pad_end, jnp.arange(n_blocks) * MOE_ROW_BLOCK, side='right'), N_EXPERTS - 1)
    x_src = jnp.concatenate([x, jnp.zeros((1, d), x.dtype)], axis=0)

    def expert_block(args):
        tok, g, e = args
        xb = x_src[tok]
        hg, hu = jnp.split(xb @ w_exp_gu[e], 2, axis=-1)
        return ((jax.nn.silu(hg) * hu) @ w_exp_down[e]) * g[:, None].astype(x.dtype)

    y_rows = lax.map(expert_block, (row_tok.reshape(n_blocks, MOE_ROW_BLOCK),
                                    row_gate.reshape(n_blocks, MOE_ROW_BLOCK), block_expert))
    routed = jax.ops.segment_sum(y_rows.reshape(rows, d), row_tok, num_segments=n + 1)[:n]
    sg, su = jnp.split(x @ w_sh_gu, 2, axis=-1)
    shared = (jax.nn.silu(sg) * su) @ w_sh_down
    return routed + shared


def decoder_layer(x, p, past_k_sb, past_v_sb, past_k_mb, past_v_mb,
                  w_in, w_branch_sb, w_branch_moba, w_out, ln1_g, ln1_b,
                  w_router, b_router, w_exp_gu, w_exp_down, w_sh_gu, w_sh_down,
                  w_ple_gate, w_ple_proj, ln2_g, ln2_b):
    b, t, _ = x.shape
    start = past_k_sb.shape[1]
    pos = start + jnp.arange(t)
    splits = [SB_WIDTH, 2 * SB_WIDTH, 3 * SB_WIDTH, 3 * SB_WIDTH + MOBA_WIDTH,
              3 * SB_WIDTH + 2 * MOBA_WIDTH, 3 * SB_WIDTH + 3 * MOBA_WIDTH,
              3 * SB_WIDTH + 3 * MOBA_WIDTH + D_MODEL]
    q_sb, k_sb, v_sb, q_mb, k_mb, v_mb, g_sb, g_mb = jnp.split(x @ w_in, splits, axis=-1)
    q_sb = q_sb.reshape(b, t, SB_HEADS, HEAD_DIM)
    k_sb = k_sb.reshape(b, t, SB_HEADS, HEAD_DIM)
    v_sb = v_sb.reshape(b, t, SB_HEADS, HEAD_DIM)
    q_mb = rope(q_mb.reshape(b, t, MOBA_HEADS, HEAD_DIM), pos)
    k_mb = rope(k_mb.reshape(b, t, MOBA_HEADS, HEAD_DIM), pos)
    v_mb = v_mb.reshape(b, t, MOBA_HEADS, HEAD_DIM)

    o_sb = stick_breaking_attention(q_sb, jnp.concatenate([past_k_sb, k_sb], axis=1),
                                    jnp.concatenate([past_v_sb, v_sb], axis=1), start)
    o_mb = moba_attention(q_mb, jnp.concatenate([past_k_mb, k_mb], axis=1),
                          jnp.concatenate([past_v_mb, v_mb], axis=1), start)
    merged = (jax.nn.sigmoid(g_sb) * (o_sb @ w_branch_sb)
              + jax.nn.sigmoid(g_mb) * (o_mb @ w_branch_moba))
    h1 = layer_norm(DEEPNORM_ALPHA * x + merged @ w_out, ln1_g, ln1_b)

    ffn = moe_ffn(h1.reshape(b * t, D_MODEL), w_router, b_router, w_exp_gu, w_exp_down,
                  w_sh_gu, w_sh_down).reshape(b, t, D_MODEL)
    ple = jax.nn.sigmoid(h1 @ w_ple_gate) * (p @ w_ple_proj)
    y = layer_norm(DEEPNORM_ALPHA * h1 + ffn + ple, ln2_g, ln2_b)

    k_new = jnp.concatenate([k_sb, k_mb], axis=2)
    v_new = jnp.concatenate([v_sb, v_mb], axis=2)
    return y, k_new, v_new


def gather_past(cache, i, page_table, lo, hi):
    rows = cache[i, page_table, :, lo:hi]
    nseq, npg, ps = rows.shape[0], rows.shape[1], rows.shape[2]
    return rows.reshape(nseq, npg * ps, hi - lo, HEAD_DIM)


def setup_inputs(seed: int = 0) -> dict:
    key = jax.random.key(seed)
    ks = jax.random.split(key, 24)
    n_pages = PAST_LEN // PAGE_SIZE
    n_phys = (DEC_BATCH * n_pages * 5) // 4

    def nrm(k, shape, s):
        return jax.random.normal(k, shape, F32) * s

    x_prompt = nrm(ks[0], (BATCH, SEQ, D_MODEL), 1.0)
    x_sample = nrm(ks[1], (DEC_BATCH, DEC_SEQ, D_MODEL), 1.0)
    cache_k = nrm(ks[2], (DEPTH, n_phys, PAGE_SIZE, N_KV_HEADS, HEAD_DIM), 1.0)
    cache_v = nrm(ks[3], (DEPTH, n_phys, PAGE_SIZE, N_KV_HEADS, HEAD_DIM), DEEPNORM_BETA)
    page_table = jax.random.permutation(ks[4], n_phys)[: DEC_BATCH * n_pages].reshape(
        DEC_BATCH, n_pages).astype(jnp.int32)
    p_prompt = nrm(ks[5], (DEPTH, BATCH, SEQ, PLE_DIM), 1.0)
    p_sample = nrm(ks[6], (DEPTH, DEC_BATCH, DEC_SEQ, PLE_DIM), 1.0)
    col_scale = jnp.concatenate([
        jnp.ones((2 * SB_WIDTH,), F32), jnp.full((SB_WIDTH,), DEEPNORM_BETA, F32),
        jnp.ones((2 * MOBA_WIDTH,), F32), jnp.full((MOBA_WIDTH,), DEEPNORM_BETA, F32),
        jnp.ones((2 * D_MODEL,), F32)])
    w_in = nrm(ks[7], (DEPTH, D_MODEL, IN_COLS), D_MODEL ** -0.5) * col_scale
    w_branch_sb = nrm(ks[8], (DEPTH, SB_WIDTH, D_MODEL), SB_WIDTH ** -0.5)
    w_branch_moba = nrm(ks[9], (DEPTH, MOBA_WIDTH, D_MODEL), MOBA_WIDTH ** -0.5)
    w_out = nrm(ks[10], (DEPTH, D_MODEL, D_MODEL), DEEPNORM_BETA * D_MODEL ** -0.5)
    ln1_g = 1.0 + nrm(ks[11], (DEPTH, D_MODEL), 0.02)
    ln1_b = nrm(ks[12], (DEPTH, D_MODEL), 0.02)
    w_router = nrm(ks[13], (DEPTH, D_MODEL, N_EXPERTS), D_MODEL ** -0.5)
    b_router = nrm(ks[14], (DEPTH, N_EXPERTS), 0.01)
    w_exp_gu = nrm(ks[15], (DEPTH, N_EXPERTS, D_MODEL, 2 * D_EXPERT), D_MODEL ** -0.5)
    w_exp_down = nrm(ks[16], (DEPTH, N_EXPERTS, D_EXPERT, D_MODEL), DEEPNORM_BETA * D_EXPERT ** -0.5)
    w_sh_gu = nrm(ks[17], (DEPTH, D_MODEL, 2 * D_EXPERT), D_MODEL ** -0.5)
    w_sh_down = nrm(ks[18], (DEPTH, D_EXPERT, D_MODEL), DEEPNORM_BETA * D_EXPERT ** -0.5)
    w_ple_gate = nrm(ks[19], (DEPTH, D_MODEL, D_MODEL), D_MODEL ** -0.5)
    w_ple_proj = nrm(ks[20], (DEPTH, PLE_DIM, D_MODEL), PLE_DIM ** -0.5)
    ln2_g = 1.0 + nrm(ks[21], (DEPTH, D_MODEL), 0.02)
    ln2_b = nrm(ks[22], (DEPTH, D_MODEL), 0.02)
    return {'x_prompt': x_prompt, 'x_sample': x_sample, 'cache_k': cache_k, 'cache_v': cache_v,
            'page_table': page_table, 'p_prompt': p_prompt, 'p_sample': p_sample,
            'w_in': w_in, 'w_branch_sb': w_branch_sb, 'w_branch_moba': w_branch_moba,
            'w_out': w_out, 'ln1_g': ln1_g, 'ln1_b': ln1_b, 'w_router': w_router,
            'b_router': b_router, 'w_exp_gu': w_exp_gu, 'w_exp_down': w_exp_down,
            'w_sh_gu': w_sh_gu, 'w_sh_down': w_sh_down, 'w_ple_gate': w_ple_gate,
            'w_ple_proj': w_ple_proj, 'ln2_g': ln2_g, 'ln2_b': ln2_b}


def reference(x_prompt, x_sample, cache_k, cache_v, page_table, p_prompt, p_sample,
              w_in, w_branch_sb, w_branch_moba, w_out, ln1_g, ln1_b, w_router, b_router,
              w_exp_gu, w_exp_down, w_sh_gu, w_sh_down, w_ple_gate, w_ple_proj, ln2_g, ln2_b):
    h_p = x_prompt
    h_s = x_sample
    k_prompt_rows, v_prompt_rows, k_sample_rows, v_sample_rows = [], [], [], []
    for i in range(DEPTH):
        lw = (w_in[i], w_branch_sb[i], w_branch_moba[i], w_out[i], ln1_g[i], ln1_b[i],
              w_router[i], b_router[i], w_exp_gu[i], w_exp_down[i], w_sh_gu[i], w_sh_down[i],
              w_ple_gate[i], w_ple_proj[i], ln2_g[i], ln2_b[i])
        empty_sb = jnp.zeros((h_p.shape[0], 0, SB_HEADS, HEAD_DIM), h_p.dtype)
        empty_mb = jnp.zeros((h_p.shape[0], 0, MOBA_HEADS, HEAD_DIM), h_p.dtype)
        h_p, k_i, v_i = decoder_layer(h_p, p_prompt[i], empty_sb, empty_sb, empty_mb, empty_mb, *lw)
        k_prompt_rows.append(k_i)
        v_prompt_rows.append(v_i)
        past_k_sb = gather_past(cache_k, i, page_table, 0, SB_HEADS)
        past_v_sb = gather_past(cache_v, i, page_table, 0, SB_HEADS)
        past_k_mb = gather_past(cache_k, i, page_table, SB_HEADS, N_KV_HEADS)
        past_v_mb = gather_past(cache_v, i, page_table, SB_HEADS, N_KV_HEADS)
        h_s, k_i, v_i = decoder_layer(h_s, p_sample[i], past_k_sb, past_v_sb, past_k_mb, past_v_mb, *lw)
        k_sample_rows.append(k_i)
        v_sample_rows.append(v_i)
    return (h_p, h_s, jnp.stack(k_prompt_rows), jnp.stack(v_prompt_rows),
            jnp.stack(k_sample_rows), jnp.stack(v_sample_rows))
```

```python
import functools
import math

import jax
import jax.numpy as jnp
from jax import lax
from jax.experimental import pallas as pl
from jax.experimental.pallas import tpu as pltpu

F32 = jnp.float32
BF16 = jnp.bfloat16

HEAD_DIM = 64
SB_HEADS = 8
MOBA_HEADS = 8
SB_WIDTH = SB_HEADS * HEAD_DIM
MOBA_WIDTH = MOBA_HEADS * HEAD_DIM
ROPE_THETA = 10000.0
MOBA_BLOCK = 256
MOBA_TOPK = 3
EXPERT_TOPK = 8
N_EXPERT_GROUPS = 8
TOPK_GROUPS = 4
ROUTED_SCALE = 2.5
LN_EPS = 1e-5

LANES = 128
HEADS_PER_LANE_BLOCK = LANES // HEAD_DIM
ROW_TILE = 512
ATT_BLOCK = 256
MOE_ROW_BLOCK = 256
VMEM_LIMIT = 56 * 1024 * 1024
NEG_BIG = -1e30

_NT = (((1,), (1,)), ((), ()))


def _cparams(*sem):
    return pltpu.CompilerParams(dimension_semantics=sem, vmem_limit_bytes=VMEM_LIMIT)


def _split_bf16(x):
    hi = x.astype(BF16)
    lo = (x - hi.astype(F32)).astype(BF16)
    return hi, lo


def _head_mask(shape, h, lane_axis):
    lane = lax.broadcasted_iota(jnp.int32, shape, lane_axis)
    return (lane >= h * HEAD_DIM) & (lane < (h + 1) * HEAD_DIM)


def _rope_lanes(y, cos, s1, s2):
    outs = []
    for c in range(y.shape[1] // LANES):
        yc = y[:, c * LANES:(c + 1) * LANES]
        outs.append(yc * cos + pltpu.roll(yc, LANES - HEAD_DIM // 2, 1) * s1
                    + pltpu.roll(yc, HEAD_DIM // 2, 1) * s2)
    return jnp.concatenate(outs, axis=1)


def _inproj_kernel(x_ref, wq_ref, wk_ref, wv_ref, wg_ref, cos_ref, s1_ref, s2_ref, *refs, prompt):
    if prompt:
        wvt_ref, kf_ref, vf_ref, qb_ref, kb_ref, g_ref, vt_ref, km_ref = refs
    else:
        kf_ref, vf_ref, qb_ref, kb_ref, g_ref = refs
    xb = x_ref[...].astype(BF16)
    cos, s1, s2 = cos_ref[...], s1_ref[...], s2_ref[...]
    scale = 1.0 / math.sqrt(HEAD_DIM)

    yq = jnp.dot(xb, wq_ref[...], preferred_element_type=F32)
    qb_ref[:, :SB_WIDTH] = (yq[:, :SB_WIDTH] * scale).astype(BF16)
    qb_ref[:, SB_WIDTH:] = (_rope_lanes(yq[:, SB_WIDTH:], cos, s1, s2) * scale).astype(BF16)

    yk = jnp.dot(xb, wk_ref[...], preferred_element_type=F32)
    k_mb = _rope_lanes(yk[:, SB_WIDTH:], cos, s1, s2)
    kf_ref[:, :SB_WIDTH] = yk[:, :SB_WIDTH]
    kf_ref[:, SB_WIDTH:] = k_mb
    kb_ref[:, :SB_WIDTH] = yk[:, :SB_WIDTH].astype(BF16)
    kb_ref[:, SB_WIDTH:] = k_mb.astype(BF16)

    vf_ref[...] = jnp.dot(xb, wv_ref[...], preferred_element_type=F32)
    g_ref[...] = jnp.dot(xb, wg_ref[...], preferred_element_type=F32).astype(BF16)

    if prompt:
        rows = xb.shape[0]
        for r in range(rows // MOBA_BLOCK):
            blk = k_mb[r * MOBA_BLOCK:(r + 1) * MOBA_BLOCK]
            km_ref[0, r:r + 1, :] = jnp.sum(blk, axis=0, keepdims=True) * (1.0 / MOBA_BLOCK)
        vt = lax.dot_general(wvt_ref[...], xb, _NT, preferred_element_type=F32)
        for r in range(rows // ATT_BLOCK):
            vt_ref[0, r] = vt[:, r * ATT_BLOCK:(r + 1) * ATT_BLOCK].astype(BF16)


def _rope_tables(positions):
    half = HEAD_DIM // 2
    inv_freq = jnp.power(ROPE_THETA, -jnp.arange(half, dtype=F32) / half)
    ang = positions.astype(F32)[:, None] * inv_freq[None, :]
    cos, sin = jnp.cos(ang), jnp.sin(ang)
    zero = jnp.zeros_like(sin)
    reps = LANES // HEAD_DIM
    cos_t = jnp.tile(jnp.concatenate([cos, cos], axis=1), (1, reps))
    s1_t = jnp.tile(jnp.concatenate([-sin, zero], axis=1), (1, reps))
    s2_t = jnp.tile(jnp.concatenate([zero, sin], axis=1), (1, reps))
    return cos_t, s1_t, s2_t


def _inproj(x2d, w, tables, seq_len, prompt):
    n, d = x2d.shape
    tm = min(ROW_TILE, n)
    assert n % tm == 0
    width = SB_WIDTH + MOBA_WIDTH
    n_tab = tables[0].shape[0] // tm
    row = lambda i: (i, 0)
    const = lambda i: (0, 0)
    tab = lambda i: (i % n_tab, 0)
    in_specs = [pl.BlockSpec((tm, d), row),
                pl.BlockSpec((d, width), const), pl.BlockSpec((d, width), const),
                pl.BlockSpec((d, width), const), pl.BlockSpec((d, 2 * d), const),
                pl.BlockSpec((tm, LANES), tab), pl.BlockSpec((tm, LANES), tab),
                pl.BlockSpec((tm, LANES), tab)]
    args = [x2d, w["wq"], w["wk"], w["wv"], w["wg"], *tables]
    out_shape = [jax.ShapeDtypeStruct((n, width), F32), jax.ShapeDtypeStruct((n, width), F32),
                 jax.ShapeDtypeStruct((n, width), BF16), jax.ShapeDtypeStruct((n, width), BF16),
                 jax.ShapeDtypeStruct((n, 2 * d), BF16)]
    out_specs = [pl.BlockSpec((tm, width), row)] * 4 + [pl.BlockSpec((tm, 2 * d), row)]
    if prompt:
        assert tm % ATT_BLOCK == 0 and seq_len % tm == 0
        per_seq = seq_len // tm
        in_specs.append(pl.BlockSpec((width, d), const))
        args.append(w["wvt"])
        out_shape += [jax.ShapeDtypeStruct((n // seq_len, seq_len // ATT_BLOCK, width, ATT_BLOCK), BF16),
                      jax.ShapeDtypeStruct((n // tm, tm // MOBA_BLOCK, MOBA_WIDTH), F32)]
        out_specs += [pl.BlockSpec((1, tm // ATT_BLOCK, width, ATT_BLOCK),
                                   lambda i: (i // per_seq, i % per_seq, 0, 0)),
                      pl.BlockSpec((1, tm // MOBA_BLOCK, MOBA_WIDTH), lambda i: (i, 0, 0))]
    return pl.pallas_call(
        functools.partial(_inproj_kernel, prompt=prompt),
        grid=(n // tm,), in_specs=in_specs, out_specs=out_specs, out_shape=out_shape,
        compiler_params=_cparams("parallel"), name="inproj_prompt" if prompt else "inproj_sample",
    )(*args)


def _sb_block(z_t, carry, acc, v_rhs, strict, upper, v_is_transposed):
    e = jnp.exp(-jnp.abs(z_t))
    log_keep = -(jnp.maximum(z_t, 0.0) + jnp.log(1.0 + e))
    if strict is not None:
        log_keep = jnp.where(strict, log_keep, 0.0)
    hi, lo = _split_bf16(log_keep)
    after = (jnp.dot(upper, hi, preferred_element_type=F32)
             + jnp.dot(upper, lo, preferred_element_type=F32))
    w = jnp.exp(z_t + log_keep + after + carry)
    if strict is not None:
        w = jnp.where(strict, w, 0.0)
    if v_is_transposed:
        acc = acc + jnp.dot(v_rhs, w.astype(BF16), preferred_element_type=F32)
    else:
        acc = acc + jnp.dot(w.T.astype(BF16), v_rhs, preferred_element_type=F32)
    carry = carry + jnp.sum(log_keep, axis=0, keepdims=True)
    return carry, acc


def _strict_upper(n):
    r = lax.broadcasted_iota(jnp.int32, (n, n), 0)
    c = lax.broadcasted_iota(jnp.int32, (n, n), 1)
    return r < c


def _sb_prompt_kernel(q_ref, k_ref, vt_ref, o_ref):
    i = pl.program_id(2)
    tb = q_ref.shape[1]
    q = q_ref[0]
    strict = _strict_upper(tb)
    upper = jnp.where(strict, 1.0, 0.0).astype(BF16)
    heads = []
    for h in range(HEADS_PER_LANE_BLOCK):
        qh = jnp.where(_head_mask(q.shape, h, 1), q, jnp.zeros_like(q))

        def step(j, state, masked, qh=qh):
            carry, acc = state
            kb = k_ref[0, pl.ds(pl.multiple_of(j * tb, tb), tb), :]
            z_t = lax.dot_general(kb, qh, _NT, preferred_element_type=F32)
            return _sb_block(z_t, carry, acc, vt_ref[0, j], strict if masked else None, upper, True)

        state = (jnp.zeros((1, tb), F32), jnp.zeros((LANES, tb), F32))
        state = step(i, state, True)
        state = lax.fori_loop(0, i, lambda jj, st: step(i - 1 - jj, st, False), state)
        heads.append(state[1])
    rows = lax.broadcasted_iota(jnp.int32, (LANES, tb), 0)
    acc_t = jnp.where(rows < HEAD_DIM, heads[0], heads[1])
    o_ref[0] = acc_t.T.astype(o_ref.dtype)


def _sb_prompt(qb, kb, vt):
    b, t, _ = qb.shape
    tb = ATT_BLOCK
    n_pairs = SB_WIDTH // LANES
    return pl.pallas_call(
        _sb_prompt_kernel,
        grid=(b, n_pairs, t // tb),
        in_specs=[pl.BlockSpec((1, tb, LANES), lambda bi, hp, i: (bi, i, hp)),
                  pl.BlockSpec((1, t, LANES), lambda bi, hp, i: (bi, 0, hp)),
                  pl.BlockSpec((1, t // tb, LANES, tb), lambda bi, hp, i: (bi, 0, hp, 0))],
        out_specs=pl.BlockSpec((1, tb, LANES), lambda bi, hp, i: (bi, i, hp)),
        out_shape=jax.ShapeDtypeStruct((b, t, SB_WIDTH), BF16),
        compiler_params=_cparams("parallel", "parallel", "parallel"), name="sb_prompt",
    )(qb, kb, vt)


def _sb_sample_kernel(pt_ref, q_ref, kn_ref, vn_ref, kc_ref, vc_ref, o_ref, carry_ref, acc_ref, *, n_new):
    del pt_ref
    s = pl.program_id(1)
    q = q_ref[0]
    n = kc_ref.shape[1]
    strict_later = _strict_upper(n)
    upper = jnp.where(strict_later, 1.0, 0.0).astype(BF16)

    def visit(k_blk, v_blk, strict):
        z_t = lax.dot_general(k_blk.astype(BF16), q, _NT, preferred_element_type=F32)
        carry, acc = _sb_block(z_t, carry_ref[...], acc_ref[...], v_blk.astype(BF16), strict, upper, False)
        carry_ref[...] = carry
        acc_ref[...] = acc

    @pl.when(s == 0)
    def _():
        carry_ref[...] = jnp.zeros_like(carry_ref)
        acc_ref[...] = jnp.zeros_like(acc_ref)
        key = lax.broadcasted_iota(jnp.int32, (n, q.shape[0]), 0)
        tok = lax.broadcasted_iota(jnp.int32, (n, q.shape[0]), 1) % n_new
        visit(kn_ref[0], vn_ref[0], key < tok)

    @pl.when(s > 0)
    def _():
        visit(kc_ref[0], vc_ref[0], None)

    @pl.when(s == pl.num_programs(1) - 1)
    def _():
        o_ref[0] = acc_ref[...]


def _sb_sample(page_table, qexp, k_new, v_new, cache_k, cache_v, n_new):
    bs, cols, width = qexp.shape
    n_pages = page_table.shape[1]
    page = cache_k.shape[1]
    page_of = lambda s: n_pages - 1 - jnp.maximum(s - 1, 0)
    grid_spec = pltpu.PrefetchScalarGridSpec(
        num_scalar_prefetch=1, grid=(bs, n_pages + 1),
        in_specs=[pl.BlockSpec((1, cols, width), lambda b, s, pt: (b, 0, 0)),
                  pl.BlockSpec((1, page, width), lambda b, s, pt: (b, 0, 0)),
                  pl.BlockSpec((1, page, width), lambda b, s, pt: (b, 0, 0)),
                  pl.BlockSpec((1, page, width), lambda b, s, pt: (pt[b, page_of(s)], 0, 0)),
                  pl.BlockSpec((1, page, width), lambda b, s, pt: (pt[b, page_of(s)], 0, 0))],
        out_specs=pl.BlockSpec((1, cols, width), lambda b, s, pt: (b, 0, 0)),
        scratch_shapes=[pltpu.VMEM((1, cols), F32), pltpu.VMEM((cols, width), F32)])
    return pl.pallas_call(
        functools.partial(_sb_sample_kernel, n_new=n_new),
        grid_spec=grid_spec, out_shape=jax.ShapeDtypeStruct((bs, cols, width), F32),
        compiler_params=_cparams("parallel", "arbitrary"), name="sb_sample",
    )(page_table, qexp, k_new, v_new, cache_k, cache_v)


def _top_rows(gate, valid, n_sel, axis):
    idx = lax.broadcasted_iota(jnp.int32, gate.shape, axis)
    size = gate.shape[axis]
    cur = jnp.where(valid, gate, -jnp.inf)
    sel = jnp.zeros(gate.shape, jnp.bool_)
    for _ in range(n_sel):
        m = jnp.max(cur, axis=axis, keepdims=True)
        first = jnp.min(jnp.where(cur == m, idx, size), axis=axis, keepdims=True)
        pick = (idx == first) & valid
        sel = sel | pick
        cur = jnp.where(pick, -jnp.inf, cur)
    return sel


def _moba_prompt_kernel(q_ref, k_ref, vt_ref, km_ref, o_ref, sel_ref):
    i = pl.program_id(2)
    tb = q_ref.shape[1]
    nb = km_ref.shape[1]
    q = q_ref[0]
    km_hi, km_lo = _split_bf16(km_ref[0])
    key = lax.broadcasted_iota(jnp.int32, (tb, tb), 0)
    qry = lax.broadcasted_iota(jnp.int32, (tb, tb), 1)
    causal = key <= qry
    blk = lax.broadcasted_iota(jnp.int32, (nb, tb), 0)
    heads = []
    for h in range(HEADS_PER_LANE_BLOCK):
        qh = jnp.where(_head_mask(q.shape, h, 1), q, jnp.zeros_like(q))
        gate = (lax.dot_general(km_hi, qh, _NT, preferred_element_type=F32)
                + lax.dot_general(km_lo, qh, _NT, preferred_element_type=F32))
        sel = _top_rows(gate, blk < i, MOBA_TOPK, 0)
        sel_ref[h, 0:nb, :] = jnp.where(sel, 1.0, 0.0)

        def scores(j, qh=qh):
            kb = k_ref[0, pl.ds(pl.multiple_of(j * tb, tb), tb), :]
            return lax.dot_general(kb, qh, _NT, preferred_element_type=F32)

        s_t = jnp.where(causal, scores(i), -jnp.inf)
        m = jnp.max(s_t, axis=0, keepdims=True)
        p = jnp.exp(s_t - m)
        l = jnp.sum(p, axis=0, keepdims=True)
        acc = jnp.dot(vt_ref[0, i], p.astype(BF16), preferred_element_type=F32)

        def step(j, state, h=h, scores=scores):
            m, l, acc = state
            picked = sel_ref[h, pl.ds(j, 1), :] > 0.5
            s_t = jnp.where(picked, scores(j), -jnp.inf)
            m_new = jnp.maximum(m, jnp.max(s_t, axis=0, keepdims=True))
            alpha = jnp.exp(m - m_new)
            p = jnp.exp(s_t - m_new)
            l = alpha * l + jnp.sum(p, axis=0, keepdims=True)
            acc = alpha * acc + jnp.dot(vt_ref[0, j], p.astype(BF16), preferred_element_type=F32)
            return m_new, l, acc

        m, l, acc = lax.fori_loop(0, i, step, (m, l, acc))
        heads.append(acc / l)
    rows = lax.broadcasted_iota(jnp.int32, (LANES, tb), 0)
    out_t = jnp.where(rows < HEAD_DIM, heads[0], heads[1])
    o_ref[0] = out_t.T.astype(o_ref.dtype)


def _moba_prompt(qb, kb, vt, kmean):
    b, t, _ = qb.shape
    tb = ATT_BLOCK
    nb = t // MOBA_BLOCK
    n_pairs = MOBA_WIDTH // LANES
    off = SB_WIDTH // LANES
    return pl.pallas_call(
        _moba_prompt_kernel,
        grid=(b, n_pairs, t // tb),
        in_specs=[pl.BlockSpec((1, tb, LANES), lambda bi, hp, i: (bi, i, off + hp)),
                  pl.BlockSpec((1, t, LANES), lambda bi, hp, i: (bi, 0, off + hp)),
                  pl.BlockSpec((1, t // tb, LANES, tb), lambda bi, hp, i: (bi, 0, off + hp, 0)),
                  pl.BlockSpec((1, nb, LANES), lambda bi, hp, i: (bi, 0, hp))],
        out_specs=pl.BlockSpec((1, tb, LANES), lambda bi, hp, i: (bi, i, hp)),
        out_shape=jax.ShapeDtypeStruct((b, t, MOBA_WIDTH), BF16),
        scratch_shapes=[pltpu.VMEM((HEADS_PER_LANE_BLOCK, -(-nb // 8) * 8, tb), F32)],
        compiler_params=_cparams("parallel", "parallel", "parallel"), name="moba_prompt",
    )(qb, kb, vt, kmean)


def _moba_sample_kernel(pt_ref, q_ref, kn_ref, vn_ref, ka_ref, kb_ref, va_ref, vb_ref, o_ref,
                        m_ref, l_ref, km_ref, acc_ref, *, n_new, n_rows):
    del pt_ref
    s = pl.program_id(1)
    nbp = pl.num_programs(1) - 1
    q = q_ref[0]
    cols = q.shape[0]

    @pl.when(s == 0)
    def _():
        m_ref[...] = jnp.full(m_ref.shape, NEG_BIG, F32)
        l_ref[...] = jnp.zeros_like(l_ref)
        km_ref[...] = jnp.zeros_like(km_ref)

    @pl.when(s < nbp)
    def _():
        k = jnp.concatenate([ka_ref[0], kb_ref[0]], axis=0)
        v = jnp.concatenate([va_ref[0], vb_ref[0]], axis=0)
        s_t = lax.dot_general(k.astype(BF16), q, _NT, preferred_element_type=F32)
        m = jnp.max(s_t, axis=0, keepdims=True)
        p = jnp.exp(s_t - m)
        acc = jnp.dot(p.T.astype(BF16), v.astype(BF16), preferred_element_type=F32)
        acc_ref[s] = acc[0:n_rows]
        m_ref[pl.ds(s, 1), :] = m
        l_ref[pl.ds(s, 1), :] = jnp.sum(p, axis=0, keepdims=True)
        km_ref[pl.ds(s, 1), :] = jnp.sum(k, axis=0, keepdims=True) * (1.0 / k.shape[0])

    @pl.when(s == nbp)
    def _():
        km_hi, km_lo = _split_bf16(km_ref[...])
        gate = (lax.dot_general(q, km_hi, _NT, preferred_element_type=F32)
                + lax.dot_general(q, km_lo, _NT, preferred_element_type=F32))
        blk = lax.broadcasted_iota(jnp.int32, gate.shape, 1)
        sel = _top_rows(gate, blk < nbp, MOBA_TOPK, 1)
        m_t = m_ref[...].T
        l_t = l_ref[...].T
        kn = kn_ref[0].astype(BF16)
        s_own = lax.dot_general(q, kn, _NT, preferred_element_type=F32)
        key = lax.broadcasted_iota(jnp.int32, s_own.shape, 1)
        tok = lax.broadcasted_iota(jnp.int32, s_own.shape, 0) % n_new
        s_own = jnp.where(key <= tok, s_own, -jnp.inf)
        m_tot = jnp.maximum(jnp.max(s_own, axis=1, keepdims=True),
                            jnp.max(jnp.where(sel, m_t, -jnp.inf), axis=1, keepdims=True))
        a = jnp.where(sel, jnp.exp(m_t - m_tot), 0.0)
        p_own = jnp.exp(s_own - m_tot)
        denom = jnp.sum(a * l_t, axis=1, keepdims=True) + jnp.sum(p_own, axis=1, keepdims=True)
        num = jnp.dot(p_own.astype(BF16), vn_ref[0].astype(BF16), preferred_element_type=F32)[0:n_rows]
        for j in range(acc_ref.shape[0]):
            num = num + a[0:n_rows, j:j + 1] * acc_ref[j]
        o_ref[0] = num / denom[0:n_rows]


def _moba_sample(page_table, qexp, k_new, v_new, cache_k, cache_v, n_new, n_rows):
    bs, cols, width = qexp.shape
    n_pages = page_table.shape[1]
    page = cache_k.shape[1]
    per_blk = MOBA_BLOCK // page
    assert per_blk == 2 and n_pages % per_blk == 0
    nbp = n_pages // per_blk
    assert nbp <= cols
    half = SB_WIDTH // width

    def page_spec(which):
        return pl.BlockSpec((1, page, width),
                            lambda b, s, pt: (pt[b, per_blk * jnp.minimum(s, nbp - 1) + which], 0, half))

    grid_spec = pltpu.PrefetchScalarGridSpec(
        num_scalar_prefetch=1, grid=(bs, nbp + 1),
        in_specs=[pl.BlockSpec((1, cols, width), lambda b, s, pt: (b, 0, 0)),
                  pl.BlockSpec((1, cols, width), lambda b, s, pt: (b, 0, 0)),
                  pl.BlockSpec((1, cols, width), lambda b, s, pt: (b, 0, 0)),
                  page_spec(0), page_spec(1), page_spec(0), page_spec(1)],
        out_specs=pl.BlockSpec((1, n_rows, width), lambda b, s, pt: (b, 0, 0)),
        scratch_shapes=[pltpu.VMEM((cols, cols), F32), pltpu.VMEM((cols, cols), F32),
                        pltpu.VMEM((cols, width), F32), pltpu.VMEM((nbp, n_rows, width), F32)])
    return pl.pallas_call(
        functools.partial(_moba_sample_kernel, n_new=n_new, n_rows=n_rows),
        grid_spec=grid_spec, out_shape=jax.ShapeDtypeStruct((bs, n_rows, width), F32),
        compiler_params=_cparams("parallel", "arbitrary"), name="moba_sample",
    )(page_table, qexp, k_new, v_new, cache_k, cache_k, cache_v, cache_v)


def _layer_norm(t, g, b):
    mu = jnp.mean(t, axis=-1, keepdims=True)
    tc = t - mu
    var = jnp.mean(tc * tc, axis=-1, keepdims=True)
    return tc * lax.rsqrt(var + LN_EPS) * g + b


def _post_attn_kernel(osb_ref, omb_ref, g_ref, x_ref, wsb_ref, wmb_ref, wout_ref, lg_ref, lb_ref,
                      h_ref, hb_ref, *, alpha):
    d = x_ref.shape[1]
    bsb = jnp.dot(osb_ref[...], wsb_ref[...], preferred_element_type=F32)
    bmb = jnp.dot(omb_ref[...], wmb_ref[...], preferred_element_type=F32)
    merged = (jax.nn.sigmoid(g_ref[:, :d].astype(F32)) * bsb
              + jax.nn.sigmoid(g_ref[:, d:].astype(F32)) * bmb)
    t = alpha * x_ref[...] + jnp.dot(merged.astype(BF16), wout_ref[...], preferred_element_type=F32)
    h = _layer_norm(t, lg_ref[...], lb_ref[...])
    h_ref[...] = h
    hb_ref[...] = h.astype(BF16)


def _post_attn(osb, omb, g, x2d, w, alpha):
    n, d = x2d.shape
    tm = min(ROW_TILE, n)
    row = lambda i: (i, 0)
    const = lambda i: (0, 0)
    return pl.pallas_call(
        functools.partial(_post_attn_kernel, alpha=alpha),
        grid=(n // tm,),
        in_specs=[pl.BlockSpec((tm, SB_WIDTH), row), pl.BlockSpec((tm, MOBA_WIDTH), row),
                  pl.BlockSpec((tm, 2 * d), row), pl.BlockSpec((tm, d), row),
                  pl.BlockSpec((SB_WIDTH, d), const), pl.BlockSpec((MOBA_WIDTH, d), const),
                  pl.BlockSpec((d, d), const), pl.BlockSpec((1, d), const), pl.BlockSpec((1, d), const)],
        out_specs=[pl.BlockSpec((tm, d), row), pl.BlockSpec((tm, d), row)],
        out_shape=[jax.ShapeDtypeStruct((n, d), F32), jax.ShapeDtypeStruct((n, d), BF16)],
        compiler_params=_cparams("parallel"), name="post_attn",
    )(osb, omb, g, x2d, w["wsb"], w["wmb"], w["wout"], w["ln1_g"], w["ln1_b"])


def _router_kernel(h_ref, wh_ref, wl_ref, b_ref, e_ref, g_ref):
    hh, hl = _split_bf16(h_ref[...])
    logits = (lax.dot_general(wh_ref[...], hh, _NT, preferred_element_type=F32)
              + lax.dot_general(wh_ref[...], hl, _NT, preferred_element_type=F32)
              + lax.dot_general(wl_ref[...], hh, _NT, preferred_element_type=F32))
    scores = jax.nn.sigmoid(logits)
    biased = scores + b_ref[...]
    n_exp, tm = scores.shape
    per_group = n_exp // N_EXPERT_GROUPS
    group_scores = []
    for g in range(N_EXPERT_GROUPS):
        v = biased[g * per_group:(g + 1) * per_group]
        top2 = _top_rows(v, jnp.ones(v.shape, jnp.bool_), 2, 0)
        group_scores.append(jnp.sum(jnp.where(top2, v, 0.0), axis=0, keepdims=True))
    group_scores = jnp.concatenate(group_scores, axis=0)
    keep = _top_rows(group_scores, jnp.ones(group_scores.shape, jnp.bool_), TOPK_GROUPS, 0)
    masked = jnp.concatenate(
        [jnp.where(keep[g:g + 1, :], biased[g * per_group:(g + 1) * per_group], -jnp.inf)
         for g in range(N_EXPERT_GROUPS)], axis=0)
    row = lax.broadcasted_iota(jnp.int32, masked.shape, 0)
    idxs, gates = [], []
    for _ in range(EXPERT_TOPK):
        m = jnp.max(masked, axis=0, keepdims=True)
        first = jnp.min(jnp.where(masked == m, row, n_exp), axis=0, keepdims=True)
        pick = row == first
        idxs.append(first)
        gates.append(jnp.sum(jnp.where(pick, scores, 0.0), axis=0, keepdims=True))
        masked = jnp.where(pick, -jnp.inf, masked)
    gate = jnp.concatenate(gates, axis=0)
    gate = gate / jnp.sum(gate, axis=0, keepdims=True) * ROUTED_SCALE
    e_ref[...] = jnp.concatenate(idxs, axis=0)
    g_ref[...] = gate


def _router(h, w):
    n, d = h.shape
    tm = min(ROW_TILE, n)
    n_exp = w["wr_hi"].shape[0]
    const = lambda i: (0, 0)
    return pl.pallas_call(
        _router_kernel, grid=(n // tm,),
        in_specs=[pl.BlockSpec((tm, d), lambda i: (i, 0)), pl.BlockSpec((n_exp, d), const),
                  pl.BlockSpec((n_exp, d), const), pl.BlockSpec((n_exp, 1), const)],
        out_specs=[pl.BlockSpec((EXPERT_TOPK, tm), lambda i: (0, i))] * 2,
        out_shape=[jax.ShapeDtypeStruct((EXPERT_TOPK, n), jnp.int32),
                   jax.ShapeDtypeStruct((EXPERT_TOPK, n), F32)],
        compiler_params=_cparams("parallel"), name="router",
    )(h, w["wr_hi"], w["wr_lo"], w["b_router"])


def _experts_kernel(be_ref, nu_ref, x_ref, gate_ref, wgu_ref, wdn_ref, y_ref):
    del be_ref
    i = pl.program_id(0)

    @pl.when(i < nu_ref[0])
    def _():
        de = wdn_ref.shape[1]
        hgu = jnp.dot(x_ref[...], wgu_ref[0].astype(BF16), preferred_element_type=F32)
        act = jax.nn.silu(hgu[:, :de]) * hgu[:, de:]
        y = jnp.dot(act.astype(BF16), wdn_ref[0].astype(BF16), preferred_element_type=F32)
        y_ref[...] = (y * gate_ref[...]).astype(y_ref.dtype)

    @pl.when(i >= nu_ref[0])
    def _():
        y_ref[...] = jnp.zeros_like(y_ref)


def _experts(block_expert, n_used, xs, row_gate, w_gu, w_dn):
    rows, d = xs.shape
    rb = MOE_ROW_BLOCK
    de2 = w_gu.shape[2]
    grid_spec = pltpu.PrefetchScalarGridSpec(
        num_scalar_prefetch=2, grid=(rows // rb,),
        in_specs=[pl.BlockSpec((rb, d), lambda i, be, nu: (i, 0)),
                  pl.BlockSpec((rb, 1), lambda i, be, nu: (i, 0)),
                  pl.BlockSpec((1, d, de2), lambda i, be, nu: (be[i], 0, 0)),
                  pl.BlockSpec((1, de2 // 2, d), lambda i, be, nu: (be[i], 0, 0))],
        out_specs=pl.BlockSpec((rb, d), lambda i, be, nu: (i, 0)))
    return pl.pallas_call(
        _experts_kernel, grid_spec=grid_spec, out_shape=jax.ShapeDtypeStruct((rows, d), BF16),
        compiler_params=_cparams("arbitrary"), name="experts",
    )(block_expert, n_used, xs, row_gate, w_gu, w_dn)


def _final_kernel(h_ref, hb_ref, r_ref, p_ref, wsg_ref, wsd_ref, wpg_ref, wpp_ref, lg_ref, lb_ref,
                  y_ref, *, alpha):
    hb = hb_ref[...]
    de = wsd_ref.shape[0]
    sgu = jnp.dot(hb, wsg_ref[...], preferred_element_type=F32)
    shared = jnp.dot((jax.nn.silu(sgu[:, :de]) * sgu[:, de:]).astype(BF16), wsd_ref[...],
                     preferred_element_type=F32)
    ple = (jax.nn.sigmoid(jnp.dot(hb, wpg_ref[...], preferred_element_type=F32))
           * jnp.dot(p_ref[...].astype(BF16), wpp_ref[...], preferred_element_type=F32))
    t = alpha * h_ref[...] + (r_ref[...] + shared) + ple
    y_ref[...] = _layer_norm(t, lg_ref[...], lb_ref[...])


def _final(h, hb, routed, p2d, w, alpha):
    n, d = h.shape
    tm = min(ROW_TILE, n)
    pd = p2d.shape[1]
    de2 = w["wsg"].shape[1]
    row = lambda i: (i, 0)
    const = lambda i: (0, 0)
    return pl.pallas_call(
        functools.partial(_final_kernel, alpha=alpha),
        grid=(n // tm,),
        in_specs=[pl.BlockSpec((tm, d), row), pl.BlockSpec((tm, d), row), pl.BlockSpec((tm, d), row),
                  pl.BlockSpec((tm, pd), row),
                  pl.BlockSpec((d, de2), const), pl.BlockSpec((de2 // 2, d), const),
                  pl.BlockSpec((d, d), const), pl.BlockSpec((pd, d), const),
                  pl.BlockSpec((1, d), const), pl.BlockSpec((1, d), const)],
        out_specs=pl.BlockSpec((tm, d), row),
        out_shape=jax.ShapeDtypeStruct((n, d), F32),
        compiler_params=_cparams("parallel"), name="final",
    )(h, hb, routed, p2d, w["wsg"], w["wsd"], w["wpg"], w["wpp"], w["ln2_g"], w["ln2_b"])


def _moe_routed(hb, top_e, gate, w_gu, w_dn):
    n, d = hb.shape
    n_exp = w_gu.shape[0]
    k = top_e.shape[1]
    rb = MOE_ROW_BLOCK
    n_assign = n * k
    flat_e = top_e.reshape(-1)
    order = jnp.argsort(flat_e)
    sorted_tok = (order // k).astype(jnp.int32)
    sorted_gate = gate.reshape(-1)[order]
    counts = jnp.bincount(flat_e, length=n_exp)
    padded = (counts + rb - 1) // rb * rb
    pad_end = jnp.cumsum(padded)
    pad_start = pad_end - padded
    raw_start = jnp.cumsum(counts) - counts
    n_blocks = -(-(n_assign + n_exp * (rb - 1)) // rb)
    rows = n_blocks * rb
    r = jnp.arange(rows)
    row_e = jnp.minimum(jnp.searchsorted(pad_end, r, side="right"), n_exp - 1)
    local = r - pad_start[row_e]
    valid = (local < counts[row_e]) & (r < pad_end[-1])
    src = jnp.clip(raw_start[row_e] + local, 0, n_assign - 1)
    row_tok = jnp.where(valid, sorted_tok[src], n)
    row_gate = jnp.where(valid, sorted_gate[src], 0.0)
    block_expert = row_e[::rb].astype(jnp.int32)
    n_used = (pad_end[-1] // rb).astype(jnp.int32).reshape(1)
    sorted_e = flat_e[order]
    dest = pad_start[sorted_e] + jnp.arange(n_assign) - raw_start[sorted_e]
    pos = jnp.zeros((n_assign,), jnp.int32).at[order].set(dest.astype(jnp.int32))

    xs = jnp.concatenate([hb, jnp.zeros((1, d), hb.dtype)], axis=0)[row_tok]
    y = _experts(block_expert, n_used, xs, row_gate[:, None], w_gu, w_dn)
    return jnp.sum(y[pos].reshape(n, k, d).astype(F32), axis=1)


def _expand_heads(q2d, n_seq, n_new, n_heads, cols):
    q = q2d.reshape(n_seq, n_new, n_heads, HEAD_DIM)
    eye = jnp.eye(n_heads, dtype=q2d.dtype)
    qe = jnp.einsum("bthd,hg->bhtgd", q, eye).reshape(n_seq, n_heads * n_new, n_heads * HEAD_DIM)
    return jnp.pad(qe, ((0, 0), (0, cols - n_heads * n_new), (0, 0)))


def _collapse_heads(o, n_seq, n_new, n_heads):
    o = o[:, :n_heads * n_new].reshape(n_seq, n_heads, n_new, n_heads, HEAD_DIM)
    o = jnp.einsum("bhtgd,hg->bthd", o, jnp.eye(n_heads, dtype=o.dtype))
    return o.reshape(n_seq * n_new, n_heads * HEAD_DIM)


def _prep_weights(i, w_in, w_branch_sb, w_branch_moba, w_out, ln1_g, ln1_b, w_router, b_router,
                  w_sh_gu, w_sh_down, w_ple_gate, w_ple_proj, ln2_g, ln2_b):
    wi = w_in[i]
    d = wi.shape[0]
    s, m = SB_WIDTH, MOBA_WIDTH
    q_sb, k_sb, v_sb = wi[:, 0:s], wi[:, s:2 * s], wi[:, 2 * s:3 * s]
    o = 3 * s
    q_mb, k_mb, v_mb = wi[:, o:o + m], wi[:, o + m:o + 2 * m], wi[:, o + 2 * m:o + 3 * m]
    wv = jnp.concatenate([v_sb, v_mb], axis=1).astype(BF16)
    wr_hi, wr_lo = _split_bf16(w_router[i].T)
    return {
        "wq": jnp.concatenate([q_sb, q_mb], axis=1).astype(BF16),
        "wk": jnp.concatenate([k_sb, k_mb], axis=1).astype(BF16),
        "wv": wv, "wvt": wv.T, "wg": wi[:, o + 3 * m:].astype(BF16),
        "wsb": w_branch_sb[i].astype(BF16), "wmb": w_branch_moba[i].astype(BF16),
        "wout": w_out[i].astype(BF16),
        "ln1_g": ln1_g[i].reshape(1, d), "ln1_b": ln1_b[i].reshape(1, d),
        "wr_hi": wr_hi, "wr_lo": wr_lo, "b_router": b_router[i].reshape(-1, 1).astype(F32),
        "wsg": w_sh_gu[i].astype(BF16), "wsd": w_sh_down[i].astype(BF16),
        "wpg": w_ple_gate[i].astype(BF16), "wpp": w_ple_proj[i].astype(BF16),
        "ln2_g": ln2_g[i].reshape(1, d), "ln2_b": ln2_b[i].reshape(1, d),
    }


def kernel(x_prompt, x_sample, cache_k, cache_v, page_table, p_prompt, p_sample, w_in, w_branch_sb, w_branch_moba, w_out, ln1_g, ln1_b, w_router, b_router, w_exp_gu, w_exp_down, w_sh_gu, w_sh_down, w_ple_gate, w_ple_proj, ln2_g, ln2_b):
    depth = w_in.shape[0]
    alpha = (2 * depth) ** 0.25
    bp, tp, d = x_prompt.shape
    bs, ts, _ = x_sample.shape
    n_p, n_s = bp * tp, bs * ts
    n_pages = page_table.shape[1]
    page = cache_k.shape[2]
    past = n_pages * page
    width = SB_WIDTH + MOBA_WIDTH
    assert tp % ATT_BLOCK == 0 and past % MOBA_BLOCK == 0 and ts <= page
    cols = LANES
    assert SB_HEADS * ts <= cols and MOBA_HEADS * ts <= cols
    n_rows = -(-MOBA_HEADS * ts // 8) * 8

    tab_p = _rope_tables(jnp.arange(tp))
    tab_s = _rope_tables(past + (jnp.arange(n_s) % ts))

    h_p = x_prompt.reshape(n_p, d)
    h_s = x_sample.reshape(n_s, d)
    k_p_rows, v_p_rows, k_s_rows, v_s_rows = [], [], [], []
    for i in range(depth):
        w = _prep_weights(i, w_in, w_branch_sb, w_branch_moba, w_out, ln1_g, ln1_b, w_router, b_router,
                          w_sh_gu, w_sh_down, w_ple_gate, w_ple_proj, ln2_g, ln2_b)

        kf, vf, qb, kb, g_p, vt, km = _inproj(h_p, w, tab_p, tp, True)
        k_p_rows.append(kf.reshape(bp, tp, SB_HEADS + MOBA_HEADS, HEAD_DIM))
        v_p_rows.append(vf.reshape(bp, tp, SB_HEADS + MOBA_HEADS, HEAD_DIM))
        qb3, kb3 = qb.reshape(bp, tp, width), kb.reshape(bp, tp, width)
        osb_p = _sb_prompt(qb3, kb3, vt).reshape(n_p, SB_WIDTH)
        omb_p = _moba_prompt(qb3, kb3, vt, km.reshape(bp, tp // MOBA_BLOCK, MOBA_WIDTH)).reshape(n_p, MOBA_WIDTH)
        h1_p, h1b_p = _post_attn(osb_p, omb_p, g_p, h_p, w, alpha)

        kf_s, vf_s, qb_s, _, g_s = _inproj(h_s, w, tab_s, ts, False)
        k_s_rows.append(kf_s.reshape(bs, ts, SB_HEADS + MOBA_HEADS, HEAD_DIM))
        v_s_rows.append(vf_s.reshape(bs, ts, SB_HEADS + MOBA_HEADS, HEAD_DIM))
        ck = cache_k[i].reshape(cache_k.shape[1], page, width)
        cv = cache_v[i].reshape(cache_v.shape[1], page, width)
        pad_new = lambda a: jnp.pad(a.reshape(bs, ts, -1), ((0, 0), (0, page - ts), (0, 0)))
        q_sb = _expand_heads(qb_s[:, :SB_WIDTH], bs, ts, SB_HEADS, cols)
        q_mb = _expand_heads(qb_s[:, SB_WIDTH:], bs, ts, MOBA_HEADS, cols)
        o_sb = _sb_sample(page_table, q_sb, pad_new(kf_s[:, :SB_WIDTH]), pad_new(vf_s[:, :SB_WIDTH]),
                          ck, cv, ts)
        o_mb = _moba_sample(page_table, q_mb, pad_new(kf_s[:, SB_WIDTH:]), pad_new(vf_s[:, SB_WIDTH:]),
                            ck, cv, ts, n_rows)
        osb_s = _collapse_heads(o_sb, bs, ts, SB_HEADS).astype(BF16)
        omb_s = _collapse_heads(o_mb, bs, ts, MOBA_HEADS).astype(BF16)
        h1_s, h1b_s = _post_attn(osb_s, omb_s, g_s, h_s, w, alpha)

        routed = []
        for h1, h1b in ((h1_p, h1b_p), (h1_s, h1b_s)):
            e_t, g_t = _router(h1, w)
            routed.append(_moe_routed(h1b, e_t.T, g_t.T, w_exp_gu[i], w_exp_down[i]))

        h_p = _final(h1_p, h1b_p, routed[0], p_prompt[i].reshape(n_p, -1), w, alpha)
        h_s = _final(h1_s, h1b_s, routed[1], p_sample[i].reshape(n_s, -1), w, alpha)

    return (h_p.reshape(bp, tp, d), h_s.reshape(bs, ts, d),
            jnp.stack(k_p_rows), jnp.stack(v_p_rows), jnp.stack(k_s_rows), jnp.stack(v_s_rows))
```

```python
import functools
import math

import jax
import jax.numpy as jnp
from jax import lax
from jax.experimental import pallas as pl
from jax.experimental.pallas import tpu as pltpu

F32 = jnp.float32
BF16 = jnp.bfloat16

HEAD_DIM = 64
SB_HEADS = 8
MOBA_HEADS = 8
SB_WIDTH = SB_HEADS * HEAD_DIM
MOBA_WIDTH = MOBA_HEADS * HEAD_DIM
ROPE_THETA = 10000.0
MOBA_BLOCK = 256
MOBA_TOPK = 3
EXPERT_TOPK = 8
N_EXPERT_GROUPS = 8
TOPK_GROUPS = 4
ROUTED_SCALE = 2.5
LN_EPS = 1e-5

LANES = 128
SUBLANES = 8
HEADS_PER_LANE_BLOCK = LANES // HEAD_DIM
ROW_TILE = 512
ATT_BLOCK = 256
CUM_BLOCK = 256
MOE_ROW_BLOCK = 256
VMEM_LIMIT = 56 * 1024 * 1024
NEG_BIG = -1e30
SB_EXIT = -104.0

_NT = (((1,), (1,)), ((), ()))


def _cparams(*sem):
    return pltpu.CompilerParams(dimension_semantics=sem, vmem_limit_bytes=VMEM_LIMIT)


def _split_bf16(x):
    hi = x.astype(BF16)
    lo = (x - hi.astype(F32)).astype(BF16)
    return hi, lo


def _head_mask(shape, h, lane_axis):
    lane = lax.broadcasted_iota(jnp.int32, shape, lane_axis)
    return (lane >= h * HEAD_DIM) & (lane < (h + 1) * HEAD_DIM)


def _log_sigmoid_neg(z):
    return -(jnp.maximum(z, 0.0) + jnp.log(1.0 + jnp.exp(-jnp.abs(z))))


def _rope_lanes(y, cos, s1, s2):
    outs = []
    for c in range(y.shape[1] // LANES):
        yc = y[:, c * LANES:(c + 1) * LANES]
        outs.append(yc * cos + pltpu.roll(yc, LANES - HEAD_DIM // 2, 1) * s1
                    + pltpu.roll(yc, HEAD_DIM // 2, 1) * s2)
    return jnp.concatenate(outs, axis=1)


def _inproj_kernel(x_ref, wq_ref, wk_ref, wv_ref, wg_ref, cos_ref, s1_ref, s2_ref, *refs, prompt):
    if prompt:
        wvt_ref, kf_ref, vf_ref, qb_ref, kb_ref, g_ref, vt_ref, km_ref = refs
    else:
        kf_ref, vf_ref, qb_ref, g_ref = refs
    xb = x_ref[...].astype(BF16)
    cos, s1, s2 = cos_ref[...], s1_ref[...], s2_ref[...]
    scale = 1.0 / math.sqrt(HEAD_DIM)

    yq = jnp.dot(xb, wq_ref[...], preferred_element_type=F32)
    qb_ref[:, :SB_WIDTH] = (yq[:, :SB_WIDTH] * scale).astype(BF16)
    qb_ref[:, SB_WIDTH:] = (_rope_lanes(yq[:, SB_WIDTH:], cos, s1, s2) * scale).astype(BF16)

    yk = jnp.dot(xb, wk_ref[...], preferred_element_type=F32)
    k_mb = _rope_lanes(yk[:, SB_WIDTH:], cos, s1, s2)
    kf_ref[:, :SB_WIDTH] = yk[:, :SB_WIDTH]
    kf_ref[:, SB_WIDTH:] = k_mb

    vf_ref[...] = jnp.dot(xb, wv_ref[...], preferred_element_type=F32)
    g_ref[...] = jnp.dot(xb, wg_ref[...], preferred_element_type=F32).astype(BF16)

    if prompt:
        kb_ref[:, :SB_WIDTH] = yk[:, :SB_WIDTH].astype(BF16)
        kb_ref[:, SB_WIDTH:] = k_mb.astype(BF16)
        rows = xb.shape[0]
        for r in range(rows // MOBA_BLOCK):
            blk = k_mb[r * MOBA_BLOCK:(r + 1) * MOBA_BLOCK]
            km_ref[0, r:r + 1, :] = jnp.sum(blk, axis=0, keepdims=True) * (1.0 / MOBA_BLOCK)
        vt = lax.dot_general(wvt_ref[...], xb, _NT, preferred_element_type=F32)
        for r in range(rows // ATT_BLOCK):
            vt_ref[0, r] = vt[:, r * ATT_BLOCK:(r + 1) * ATT_BLOCK].astype(BF16)


def _rope_tables(positions):
    half = HEAD_DIM // 2
    inv_freq = jnp.power(ROPE_THETA, -jnp.arange(half, dtype=F32) / half)
    ang = positions.astype(F32)[:, None] * inv_freq[None, :]
    cos, sin = jnp.cos(ang), jnp.sin(ang)
    zero = jnp.zeros_like(sin)
    reps = LANES // HEAD_DIM
    cos_t = jnp.tile(jnp.concatenate([cos, cos], axis=1), (1, reps))
    s1_t = jnp.tile(jnp.concatenate([-sin, zero], axis=1), (1, reps))
    s2_t = jnp.tile(jnp.concatenate([zero, sin], axis=1), (1, reps))
    return cos_t, s1_t, s2_t


def _inproj(x2d, w, tables, seq_len, prompt):
    n, d = x2d.shape
    tm = min(ROW_TILE, n)
    assert n % tm == 0
    width = SB_WIDTH + MOBA_WIDTH
    n_tab = tables[0].shape[0] // tm
    row = lambda i: (i, 0)
    const = lambda i: (0, 0)
    tab = lambda i: (i % n_tab, 0)
    in_specs = [pl.BlockSpec((tm, d), row),
                pl.BlockSpec((d, width), const), pl.BlockSpec((d, width), const),
                pl.BlockSpec((d, width), const), pl.BlockSpec((d, 2 * d), const),
                pl.BlockSpec((tm, LANES), tab), pl.BlockSpec((tm, LANES), tab),
                pl.BlockSpec((tm, LANES), tab)]
    args = [x2d, w["wq"], w["wk"], w["wv"], w["wg"], *tables]
    out_shape = [jax.ShapeDtypeStruct((n, width), F32), jax.ShapeDtypeStruct((n, width), F32),
                 jax.ShapeDtypeStruct((n, width), BF16)]
    out_specs = [pl.BlockSpec((tm, width), row)] * 3
    if prompt:
        out_shape.append(jax.ShapeDtypeStruct((n, width), BF16))
        out_specs.append(pl.BlockSpec((tm, width), row))
    out_shape.append(jax.ShapeDtypeStruct((n, 2 * d), BF16))
    out_specs.append(pl.BlockSpec((tm, 2 * d), row))
    if prompt:
        assert tm % ATT_BLOCK == 0 and seq_len % tm == 0
        per_seq = seq_len // tm
        in_specs.append(pl.BlockSpec((width, d), const))
        args.append(w["wvt"])
        out_shape += [jax.ShapeDtypeStruct((n // seq_len, seq_len // ATT_BLOCK, width, ATT_BLOCK), BF16),
                      jax.ShapeDtypeStruct((n // tm, tm // MOBA_BLOCK, MOBA_WIDTH), F32)]
        out_specs += [pl.BlockSpec((1, tm // ATT_BLOCK, width, ATT_BLOCK),
                                   lambda i: (i // per_seq, i % per_seq, 0, 0)),
                      pl.BlockSpec((1, tm // MOBA_BLOCK, MOBA_WIDTH), lambda i: (i, 0, 0))]
    return pl.pallas_call(
        functools.partial(_inproj_kernel, prompt=prompt),
        grid=(n // tm,), in_specs=in_specs, out_specs=out_specs, out_shape=out_shape,
        compiler_params=_cparams("parallel"), name="inproj_prompt" if prompt else "inproj_sample",
    )(*args)


def _strict_upper(n):
    r = lax.broadcasted_iota(jnp.int32, (n, n), 0)
    c = lax.broadcasted_iota(jnp.int32, (n, n), 1)
    return r < c


def _sb_block_t(z_t, carry, acc, vt_blk, strict, upper):
    log_keep = _log_sigmoid_neg(z_t)
    if strict is not None:
        log_keep = jnp.where(strict, log_keep, 0.0)
    hi, lo = _split_bf16(log_keep)
    after = (jnp.dot(upper, hi, preferred_element_type=F32)
             + jnp.dot(upper, lo, preferred_element_type=F32))
    w = jnp.exp(z_t + log_keep + after + carry)
    if strict is not None:
        w = jnp.where(strict, w, 0.0)
    acc = acc + jnp.dot(vt_blk, w.astype(BF16), preferred_element_type=F32)
    carry = carry + jnp.sum(log_keep, axis=0, keepdims=True)
    return carry, acc


def _sb_prompt_kernel(q_ref, k_ref, vt_ref, o_ref):
    i = pl.program_id(2)
    tb = q_ref.shape[1]
    q = q_ref[0]
    strict = _strict_upper(tb)
    upper = jnp.where(strict, 1.0, 0.0).astype(BF16)
    qh = [jnp.where(_head_mask(q.shape, h, 1), q, jnp.zeros_like(q)) for h in range(HEADS_PER_LANE_BLOCK)]

    def visit(j, carries, accs, masked):
        kb = k_ref[0, pl.ds(pl.multiple_of(j * tb, tb), tb), :]
        vtb = vt_ref[0, j]
        outs = [_sb_block_t(lax.dot_general(kb, qh[h], _NT, preferred_element_type=F32),
                            carries[h], accs[h], vtb, strict if masked else None, upper)
                for h in range(HEADS_PER_LANE_BLOCK)]
        return tuple(o[0] for o in outs), tuple(o[1] for o in outs)

    zc = jnp.zeros((1, tb), F32)
    za = jnp.zeros((LANES, tb), F32)
    carries, accs = visit(i, (zc,) * HEADS_PER_LANE_BLOCK, (za,) * HEADS_PER_LANE_BLOCK, True)

    def cond(st):
        jj, carries, _ = st
        live = functools.reduce(jnp.maximum, [jnp.max(c) for c in carries]) > SB_EXIT
        return jnp.logical_and(jj < i, live)

    def body(st):
        jj, carries, accs = st
        carries, accs = visit(i - 1 - jj, carries, accs, False)
        return jj + 1, carries, accs

    _, carries, accs = lax.while_loop(cond, body, (jnp.int32(0), carries, accs))
    rows = lax.broadcasted_iota(jnp.int32, (LANES, tb), 0)
    acc_t = jnp.where(rows < HEAD_DIM, accs[0], accs[1])
    o_ref[0] = acc_t.T.astype(o_ref.dtype)


def _sb_prompt(qb, kb, vt):
    b, t, _ = qb.shape
    tb = ATT_BLOCK
    n_pairs = SB_WIDTH // LANES
    return pl.pallas_call(
        _sb_prompt_kernel,
        grid=(b, n_pairs, t // tb),
        in_specs=[pl.BlockSpec((1, tb, LANES), lambda bi, hp, i: (bi, i, hp)),
                  pl.BlockSpec((1, t, LANES), lambda bi, hp, i: (bi, 0, hp)),
                  pl.BlockSpec((1, t // tb, LANES, tb), lambda bi, hp, i: (bi, 0, hp, 0))],
        out_specs=pl.BlockSpec((1, tb, LANES), lambda bi, hp, i: (bi, i, hp)),
        out_shape=jax.ShapeDtypeStruct((b, t, SB_WIDTH), BF16),
        compiler_params=_cparams("parallel", "parallel", "parallel"), name="sb_prompt",
    )(qb, kb, vt)


def _row_head(shape, n_new):
    return lax.broadcasted_iota(jnp.int32, shape, 0) // n_new


def _lane_head(shape, n_heads):
    return lax.broadcasted_iota(jnp.int32, shape, 1) % n_heads


def _sb_rows(s, valid, carry, acc, v_rows, lower):
    bw = lower.shape[0]
    n_blk = s.shape[1] // bw
    log_keep = jnp.where(valid, _log_sigmoid_neg(s), 0.0)
    hi, lo = _split_bf16(log_keep)
    after, block_carry = [None] * n_blk, [None] * n_blk
    for b in reversed(range(n_blk)):
        sl = slice(b * bw, (b + 1) * bw)
        after[b] = (jnp.dot(hi[:, sl], lower, preferred_element_type=F32)
                    + jnp.dot(lo[:, sl], lower, preferred_element_type=F32))
        block_carry[b] = carry
        carry = carry + jnp.sum(log_keep[:, sl], axis=1, keepdims=True)
    after = jnp.concatenate([after[b] + block_carry[b] for b in range(n_blk)], axis=1)
    w = jnp.where(valid, jnp.exp(s + log_keep + after), 0.0)
    acc = acc + jnp.dot(w.astype(BF16), v_rows, preferred_element_type=F32)
    return carry, acc


def _later_lane(n):
    r = lax.broadcasted_iota(jnp.int32, (n, n), 0)
    c = lax.broadcasted_iota(jnp.int32, (n, n), 1)
    return jnp.where(r > c, 1.0, 0.0).astype(BF16)


def _sb_sample_kernel(pt_ref, q_ref, kn_ref, vn_ref, kc_hbm, vc_hbm, o_ref, kbuf, vbuf, sem,
                      *, layer, n_new, n_pages):
    b = pl.program_id(0)
    q = q_ref[0]
    rows = q.shape[0]
    _, page, heads, _ = kbuf.shape

    def copies(p, slot):
        pid = pt_ref[b, p]
        src = lambda ref: ref.at[layer, pid, pl.ds(0, page), pl.ds(0, heads), pl.ds(0, HEAD_DIM)]
        return (pltpu.make_async_copy(src(kc_hbm), kbuf.at[slot], sem.at[0, slot]),
                pltpu.make_async_copy(src(vc_hbm), vbuf.at[slot], sem.at[1, slot]))

    def start(p, slot):
        for c in copies(p, slot):
            c.start()

    def wait(p, slot):
        for c in copies(p, slot):
            c.wait()

    start(n_pages - 1, 0)

    kn = kn_ref[0].astype(BF16)
    n = kn.shape[0]
    sc = lax.dot_general(q, kn, _NT, preferred_element_type=F32)
    lane = lax.broadcasted_iota(jnp.int32, (rows, n), 1)
    tok = lax.broadcasted_iota(jnp.int32, (rows, n), 0) % n_new
    valid_new = (_lane_head((rows, n), heads) == _row_head((rows, n), n_new)) & (lane // heads < tok)
    carry, acc = _sb_rows(sc, valid_new, jnp.zeros((rows, 1), F32), jnp.zeros((rows, HEAD_DIM), F32),
                          vn_ref[0].astype(BF16), _later_lane(n))

    lower = _later_lane(CUM_BLOCK)
    valid = _lane_head((rows, page * heads), heads) == _row_head((rows, page * heads), n_new)

    def cond(st):
        k, carry, _ = st
        return jnp.logical_and(k < n_pages, jnp.max(carry) > SB_EXIT)

    def body(st):
        k, carry, acc = st
        p = n_pages - 1 - k
        slot = k % 2
        wait(p, slot)

        @pl.when(k + 1 < n_pages)
        def _():
            start(p - 1, 1 - slot)

        k2 = kbuf[slot].reshape(page * heads, HEAD_DIM).astype(BF16)
        v2 = vbuf[slot].reshape(page * heads, HEAD_DIM).astype(BF16)
        sc = lax.dot_general(q, k2, _NT, preferred_element_type=F32)
        carry, acc = _sb_rows(sc, valid, carry, acc, v2, lower)
        return k + 1, carry, acc

    k, carry, acc = lax.while_loop(cond, body, (jnp.int32(0), carry, acc))

    @pl.when(k < n_pages)
    def _():
        wait(n_pages - 1 - k, k % 2)

    o_ref[0] = acc


def _sb_sample(layer, page_table, q_rows, k_new, v_new, cache_k, cache_v, n_new):
    bs, rows, _ = q_rows.shape
    n_pages = page_table.shape[1]
    page = cache_k.shape[2]
    new_rows = k_new.shape[1]
    assert (page * SB_HEADS) % CUM_BLOCK == 0 and CUM_BLOCK % SB_HEADS == 0
    grid_spec = pltpu.PrefetchScalarGridSpec(
        num_scalar_prefetch=1, grid=(bs,),
        in_specs=[pl.BlockSpec((1, rows, HEAD_DIM), lambda b, pt: (b, 0, 0)),
                  pl.BlockSpec((1, new_rows, HEAD_DIM), lambda b, pt: (b, 0, 0)),
                  pl.BlockSpec((1, new_rows, HEAD_DIM), lambda b, pt: (b, 0, 0)),
                  pl.BlockSpec(memory_space=pl.ANY), pl.BlockSpec(memory_space=pl.ANY)],
        out_specs=pl.BlockSpec((1, rows, HEAD_DIM), lambda b, pt: (b, 0, 0)),
        scratch_shapes=[pltpu.VMEM((2, page, SB_HEADS, HEAD_DIM), F32),
                        pltpu.VMEM((2, page, SB_HEADS, HEAD_DIM), F32),
                        pltpu.SemaphoreType.DMA((2, 2))])
    return pl.pallas_call(
        functools.partial(_sb_sample_kernel, layer=layer, n_new=n_new, n_pages=n_pages),
        grid_spec=grid_spec, out_shape=jax.ShapeDtypeStruct((bs, rows, HEAD_DIM), F32),
        compiler_params=_cparams("arbitrary"), name="sb_sample",
    )(page_table, q_rows, k_new, v_new, cache_k, cache_v)


def _top_rows(gate, valid, n_sel, axis):
    idx = lax.broadcasted_iota(jnp.int32, gate.shape, axis)
    size = gate.shape[axis]
    cur = jnp.where(valid, gate, -jnp.inf)
    sel = jnp.zeros(gate.shape, jnp.bool_)
    for _ in range(n_sel):
        m = jnp.max(cur, axis=axis, keepdims=True)
        first = jnp.min(jnp.where(cur == m, idx, size), axis=axis, keepdims=True)
        pick = (idx == first) & valid
        sel = sel | pick
        cur = jnp.where(pick, -jnp.inf, cur)
    return sel


def _moba_prompt_kernel(q_ref, k_ref, vt_ref, km_ref, o_ref, sel_ref):
    i = pl.program_id(2)
    tb = q_ref.shape[1]
    nb = km_ref.shape[1]
    nh = HEADS_PER_LANE_BLOCK
    q = q_ref[0]
    km_hi, km_lo = _split_bf16(km_ref[0])
    key = lax.broadcasted_iota(jnp.int32, (tb, tb), 0)
    qry = lax.broadcasted_iota(jnp.int32, (tb, tb), 1)
    causal = key <= qry
    blk = lax.broadcasted_iota(jnp.int32, (nb, tb), 0)
    qh = [jnp.where(_head_mask(q.shape, h, 1), q, jnp.zeros_like(q)) for h in range(nh)]
    for h in range(nh):
        gate = (lax.dot_general(km_hi, qh[h], _NT, preferred_element_type=F32)
                + lax.dot_general(km_lo, qh[h], _NT, preferred_element_type=F32))
        sel_ref[h, 0:nb, :] = jnp.where(_top_rows(gate, blk < i, MOBA_TOPK, 0), 1.0, 0.0)

    def keys(j):
        return k_ref[0, pl.ds(pl.multiple_of(j * tb, tb), tb), :]

    kb, vtb = keys(i), vt_ref[0, i]
    state = []
    for h in range(nh):
        s_t = jnp.where(causal, lax.dot_general(kb, qh[h], _NT, preferred_element_type=F32), -jnp.inf)
        m = jnp.max(s_t, axis=0, keepdims=True)
        p = jnp.exp(s_t - m)
        state += [m, jnp.sum(p, axis=0, keepdims=True),
                  jnp.dot(vtb, p.astype(BF16), preferred_element_type=F32)]

    def step(j, state):
        kb, vtb = keys(j), vt_ref[0, j]
        out = []
        for h in range(nh):
            m, l, acc = state[3 * h:3 * h + 3]
            picked = sel_ref[h, pl.ds(j, 1), :] > 0.5
            s_t = jnp.where(picked, lax.dot_general(kb, qh[h], _NT, preferred_element_type=F32), -jnp.inf)
            m_new = jnp.maximum(m, jnp.max(s_t, axis=0, keepdims=True))
            alpha = jnp.exp(m - m_new)
            p = jnp.exp(s_t - m_new)
            out += [m_new, alpha * l + jnp.sum(p, axis=0, keepdims=True),
                    alpha * acc + jnp.dot(vtb, p.astype(BF16), preferred_element_type=F32)]
        return tuple(out)

    state = lax.fori_loop(0, i, step, tuple(state))
    rows = lax.broadcasted_iota(jnp.int32, (LANES, tb), 0)
    out_t = jnp.where(rows < HEAD_DIM, state[2] / state[1], state[5] / state[4])
    o_ref[0] = out_t.T.astype(o_ref.dtype)


def _moba_prompt(qb, kb, vt, kmean):
    b, t, _ = qb.shape
    tb = ATT_BLOCK
    nb = t // MOBA_BLOCK
    n_pairs = MOBA_WIDTH // LANES
    off = SB_WIDTH // LANES
    return pl.pallas_call(
        _moba_prompt_kernel,
        grid=(b, n_pairs, t // tb),
        in_specs=[pl.BlockSpec((1, tb, LANES), lambda bi, hp, i: (bi, i, off + hp)),
                  pl.BlockSpec((1, t, LANES), lambda bi, hp, i: (bi, 0, off + hp)),
                  pl.BlockSpec((1, t // tb, LANES, tb), lambda bi, hp, i: (bi, 0, off + hp, 0)),
                  pl.BlockSpec((1, nb, LANES), lambda bi, hp, i: (bi, 0, hp))],
        out_specs=pl.BlockSpec((1, tb, LANES), lambda bi, hp, i: (bi, i, hp)),
        out_shape=jax.ShapeDtypeStruct((b, t, MOBA_WIDTH), BF16),
        scratch_shapes=[pltpu.VMEM((HEADS_PER_LANE_BLOCK, -(-nb // SUBLANES) * SUBLANES, tb), F32)],
        compiler_params=_cparams("parallel", "parallel", "parallel"), name="moba_prompt",
    )(qb, kb, vt, kmean)


def _moba_sample_kernel(pt_ref, q_ref, kn_ref, vn_ref, *refs, n_new, n_seq):
    del pt_ref
    pages = [refs[4 * u:4 * u + 4] for u in range(n_seq)]
    o_ref, m_ref, l_ref, km_ref, acc_ref = refs[4 * n_seq:]
    s = pl.program_id(1)
    nbp = pl.num_programs(1) - 1
    page, heads, _ = pages[0][0].shape
    blk_lane = lax.broadcasted_iota(jnp.int32, m_ref.shape[1:], 1)

    @pl.when(s == 0)
    def _():
        m_ref[...] = jnp.full(m_ref.shape, NEG_BIG, F32)
        l_ref[...] = jnp.zeros_like(l_ref)
        km_ref[...] = jnp.zeros_like(km_ref)

    @pl.when(s < nbp)
    def _():
        for u in range(n_seq):
            ka_ref, kb_ref, va_ref, vb_ref = pages[u]
            q = q_ref[u]
            ka, kb = ka_ref[...], kb_ref[...]
            flat = lambda x: x.reshape(page * heads, HEAD_DIM).astype(BF16)
            k2 = jnp.concatenate([flat(ka), flat(kb)], axis=0)
            v2 = jnp.concatenate([flat(va_ref[...]), flat(vb_ref[...])], axis=0)
            sc = lax.dot_general(q, k2, _NT, preferred_element_type=F32)
            valid = _lane_head(sc.shape, heads) == _row_head(sc.shape, n_new)
            sc = jnp.where(valid, sc, -jnp.inf)
            m = jnp.max(sc, axis=1, keepdims=True)
            p = jnp.exp(sc - m)
            acc_ref[u, s] = jnp.dot(p.astype(BF16), v2, preferred_element_type=F32)
            m_ref[u] = jnp.where(blk_lane == s, m, m_ref[u])
            l_ref[u] = jnp.where(blk_lane == s, jnp.sum(p, axis=1, keepdims=True), l_ref[u])
            km = (jnp.sum(ka, axis=0) + jnp.sum(kb, axis=0)) * (1.0 / (2 * page))
            for h in range(heads):
                km_ref[u, h, pl.ds(s, 1), :] = km[h:h + 1, :]

    @pl.when(s == nbp)
    def _():
        row_head = _row_head(blk_lane.shape, n_new)
        for u in range(n_seq):
            q = q_ref[u]
            gate = jnp.zeros(blk_lane.shape, F32)
            for h in range(heads):
                hi, lo = _split_bf16(km_ref[u, h])
                g_h = (lax.dot_general(q, hi, _NT, preferred_element_type=F32)
                       + lax.dot_general(q, lo, _NT, preferred_element_type=F32))
                gate = jnp.where(row_head == h, g_h, gate)
            sel = _top_rows(gate, blk_lane < nbp, MOBA_TOPK, 1)
            kn = kn_ref[u].astype(BF16)
            s_own = lax.dot_general(q, kn, _NT, preferred_element_type=F32)
            lane = lax.broadcasted_iota(jnp.int32, s_own.shape, 1)
            tok = lax.broadcasted_iota(jnp.int32, s_own.shape, 0) % n_new
            own_ok = (_lane_head(s_own.shape, heads) == _row_head(s_own.shape, n_new)) & (lane // heads <= tok)
            s_own = jnp.where(own_ok, s_own, -jnp.inf)
            m_all, l_all = m_ref[u], l_ref[u]
            m_tot = jnp.maximum(jnp.max(s_own, axis=1, keepdims=True),
                                jnp.max(jnp.where(sel, m_all, -jnp.inf), axis=1, keepdims=True))
            a = jnp.where(sel, jnp.exp(m_all - m_tot), 0.0)
            p_own = jnp.exp(s_own - m_tot)
            denom = jnp.sum(a * l_all, axis=1, keepdims=True) + jnp.sum(p_own, axis=1, keepdims=True)
            num = jnp.dot(p_own.astype(BF16), vn_ref[u].astype(BF16), preferred_element_type=F32)
            for j in range(acc_ref.shape[1]):
                num = num + a[:, j:j + 1] * acc_ref[u, j]
            o_ref[u] = num / denom


def _moba_sample(layer, page_table, q_rows, k_new, v_new, cache_k, cache_v, n_new):
    bs, rows, _ = q_rows.shape
    n_pages = page_table.shape[1]
    page = cache_k.shape[2]
    new_rows = k_new.shape[1]
    per_blk = MOBA_BLOCK // page
    assert per_blk == 2 and n_pages % per_blk == 0
    nbp = n_pages // per_blk
    assert nbp <= LANES
    n_seq = 2 if bs % 2 == 0 else 1
    head_blk = SB_HEADS // MOBA_HEADS

    def page_spec(u, which):
        return pl.BlockSpec(
            (None, None, page, MOBA_HEADS, HEAD_DIM),
            lambda b, s, pt: (layer, pt[b * n_seq + u, per_blk * jnp.minimum(s, nbp - 1) + which], 0, head_blk, 0))

    seq = lambda b, s, pt: (b, 0, 0)
    page_specs, page_args = [], []
    for u in range(n_seq):
        page_specs += [page_spec(u, 0), page_spec(u, 1), page_spec(u, 0), page_spec(u, 1)]
        page_args += [cache_k, cache_k, cache_v, cache_v]
    grid_spec = pltpu.PrefetchScalarGridSpec(
        num_scalar_prefetch=1, grid=(bs // n_seq, nbp + 1),
        in_specs=[pl.BlockSpec((n_seq, rows, HEAD_DIM), seq),
                  pl.BlockSpec((n_seq, new_rows, HEAD_DIM), seq),
                  pl.BlockSpec((n_seq, new_rows, HEAD_DIM), seq)] + page_specs,
        out_specs=pl.BlockSpec((n_seq, rows, HEAD_DIM), seq),
        scratch_shapes=[pltpu.VMEM((n_seq, rows, LANES), F32), pltpu.VMEM((n_seq, rows, LANES), F32),
                        pltpu.VMEM((n_seq, MOBA_HEADS, LANES, HEAD_DIM), F32),
                        pltpu.VMEM((n_seq, nbp, rows, HEAD_DIM), F32)])
    return pl.pallas_call(
        functools.partial(_moba_sample_kernel, n_new=n_new, n_seq=n_seq),
        grid_spec=grid_spec, out_shape=jax.ShapeDtypeStruct((bs, rows, HEAD_DIM), F32),
        compiler_params=_cparams("parallel", "arbitrary"), name="moba_sample",
    )(page_table, q_rows, k_new, v_new, *page_args)


def _layer_norm(t, g, b):
    mu = jnp.mean(t, axis=-1, keepdims=True)
    tc = t - mu
    var = jnp.mean(tc * tc, axis=-1, keepdims=True)
    return tc * lax.rsqrt(var + LN_EPS) * g + b


def _post_attn_kernel(osb_ref, omb_ref, g_ref, x_ref, wsb_ref, wmb_ref, wout_ref, lg_ref, lb_ref,
                      h_ref, hb_ref, *, alpha):
    d = x_ref.shape[1]
    bsb = jnp.dot(osb_ref[...], wsb_ref[...], preferred_element_type=F32)
    bmb = jnp.dot(omb_ref[...], wmb_ref[...], preferred_element_type=F32)
    merged = (jax.nn.sigmoid(g_ref[:, :d].astype(F32)) * bsb
              + jax.nn.sigmoid(g_ref[:, d:].astype(F32)) * bmb)
    t = alpha * x_ref[...] + jnp.dot(merged.astype(BF16), wout_ref[...], preferred_element_type=F32)
    h = _layer_norm(t, lg_ref[...], lb_ref[...])
    h_ref[...] = h
    hb_ref[...] = h.astype(BF16)


def _post_attn(osb, omb, g, x2d, w, alpha):
    n, d = x2d.shape
    tm = min(ROW_TILE, n)
    row = lambda i: (i, 0)
    const = lambda i: (0, 0)
    return pl.pallas_call(
        functools.partial(_post_attn_kernel, alpha=alpha),
        grid=(n // tm,),
        in_specs=[pl.BlockSpec((tm, SB_WIDTH), row), pl.BlockSpec((tm, MOBA_WIDTH), row),
                  pl.BlockSpec((tm, 2 * d), row), pl.BlockSpec((tm, d), row),
                  pl.BlockSpec((SB_WIDTH, d), const), pl.BlockSpec((MOBA_WIDTH, d), const),
                  pl.BlockSpec((d, d), const), pl.BlockSpec((1, d), const), pl.BlockSpec((1, d), const)],
        out_specs=[pl.BlockSpec((tm, d), row), pl.BlockSpec((tm, d), row)],
        out_shape=[jax.ShapeDtypeStruct((n, d), F32), jax.ShapeDtypeStruct((n, d), BF16)],
        compiler_params=_cparams("parallel"), name="post_attn",
    )(osb, omb, g, x2d, w["wsb"], w["wmb"], w["wout"], w["ln1_g"], w["ln1_b"])


def _router_kernel(h_ref, wh_ref, wl_ref, b_ref, e_ref, g_ref):
    hh, hl = _split_bf16(h_ref[...])
    logits = (lax.dot_general(wh_ref[...], hh, _NT, preferred_element_type=F32)
              + lax.dot_general(wh_ref[...], hl, _NT, preferred_element_type=F32)
              + lax.dot_general(wl_ref[...], hh, _NT, preferred_element_type=F32))
    scores = jax.nn.sigmoid(logits)
    biased = scores + b_ref[...]
    n_exp, _ = scores.shape
    per_group = n_exp // N_EXPERT_GROUPS
    group_scores = []
    for g in range(N_EXPERT_GROUPS):
        v = biased[g * per_group:(g + 1) * per_group]
        top2 = _top_rows(v, jnp.ones(v.shape, jnp.bool_), 2, 0)
        group_scores.append(jnp.sum(jnp.where(top2, v, 0.0), axis=0, keepdims=True))
    group_scores = jnp.concatenate(group_scores, axis=0)
    keep = _top_rows(group_scores, jnp.ones(group_scores.shape, jnp.bool_), TOPK_GROUPS, 0)
    masked = jnp.concatenate(
        [jnp.where(keep[g:g + 1, :], biased[g * per_group:(g + 1) * per_group], -jnp.inf)
         for g in range(N_EXPERT_GROUPS)], axis=0)
    row = lax.broadcasted_iota(jnp.int32, masked.shape, 0)
    idxs, gates = [], []
    for _ in range(EXPERT_TOPK):
        m = jnp.max(masked, axis=0, keepdims=True)
        first = jnp.min(jnp.where(masked == m, row, n_exp), axis=0, keepdims=True)
        pick = row == first
        idxs.append(first)
        gates.append(jnp.sum(jnp.where(pick, scores, 0.0), axis=0, keepdims=True))
        masked = jnp.where(pick, -jnp.inf, masked)
    gate = jnp.concatenate(gates, axis=0)
    gate = gate / jnp.sum(gate, axis=0, keepdims=True) * ROUTED_SCALE
    e_ref[...] = jnp.concatenate(idxs, axis=0)
    g_ref[...] = gate


def _router(h, w):
    n, d = h.shape
    tm = min(ROW_TILE, n)
    n_exp = w["wr_hi"].shape[0]
    const = lambda i: (0, 0)
    return pl.pallas_call(
        _router_kernel, grid=(n // tm,),
        in_specs=[pl.BlockSpec((tm, d), lambda i: (i, 0)), pl.BlockSpec((n_exp, d), const),
                  pl.BlockSpec((n_exp, d), const), pl.BlockSpec((n_exp, 1), const)],
        out_specs=[pl.BlockSpec((EXPERT_TOPK, tm), lambda i: (0, i))] * 2,
        out_shape=[jax.ShapeDtypeStruct((EXPERT_TOPK, n), jnp.int32),
                   jax.ShapeDtypeStruct((EXPERT_TOPK, n), F32)],
        compiler_params=_cparams("parallel"), name="router",
    )(h, w["wr_hi"], w["wr_lo"], w["b_router"])


def _experts_kernel(wb_ref, we_ref, ws_ref, wt_ref, x_ref, gate_ref, wgu_ref, wdn_ref, y_ref):
    del we_ref
    w = pl.program_id(0)
    blk = wb_ref[w]
    start, end = ws_ref[w], wt_ref[w]
    rb = x_ref.shape[0]

    @pl.when(jnp.logical_or(w == 0, wb_ref[jnp.maximum(w - 1, 0)] != blk))
    def _():
        y_ref[...] = jnp.zeros_like(y_ref)

    @pl.when(end > start)
    def _():
        de = wdn_ref.shape[1]
        hgu = jnp.dot(x_ref[...], wgu_ref[0], preferred_element_type=F32)
        act = jax.nn.silu(hgu[:, :de]) * hgu[:, de:]
        y = jnp.dot(act.astype(BF16), wdn_ref[0], preferred_element_type=F32) * gate_ref[...]
        r = blk * rb + lax.broadcasted_iota(jnp.int32, (rb, 1), 0)
        mine = jnp.logical_and(r >= start, r < end)
        y_ref[...] = jnp.where(mine, y.astype(y_ref.dtype), y_ref[...])


def _experts(work, xs, row_gate, w_gu, w_dn):
    rows, d = xs.shape
    rb = MOE_ROW_BLOCK
    de2 = w_gu.shape[2]
    n_work = work[0].shape[0]
    grid_spec = pltpu.PrefetchScalarGridSpec(
        num_scalar_prefetch=4, grid=(n_work,),
        in_specs=[pl.BlockSpec((rb, d), lambda w, wb, we, ws, wt: (wb[w], 0)),
                  pl.BlockSpec((rb, 1), lambda w, wb, we, ws, wt: (wb[w], 0)),
                  pl.BlockSpec((1, d, de2), lambda w, wb, we, ws, wt: (we[w], 0, 0)),
                  pl.BlockSpec((1, de2 // 2, d), lambda w, wb, we, ws, wt: (we[w], 0, 0))],
        out_specs=pl.BlockSpec((rb, d), lambda w, wb, we, ws, wt: (wb[w], 0)))
    return pl.pallas_call(
        _experts_kernel, grid_spec=grid_spec, out_shape=jax.ShapeDtypeStruct((rows, d), BF16),
        compiler_params=_cparams("arbitrary"), name="experts",
    )(*work, xs, row_gate, w_gu, w_dn)


def _final_kernel(h_ref, hb_ref, r_ref, p_ref, wsg_ref, wsd_ref, wpg_ref, wpp_ref, lg_ref, lb_ref,
                  y_ref, *, alpha):
    hb = hb_ref[...]
    de = wsd_ref.shape[0]
    sgu = jnp.dot(hb, wsg_ref[...], preferred_element_type=F32)
    shared = jnp.dot((jax.nn.silu(sgu[:, :de]) * sgu[:, de:]).astype(BF16), wsd_ref[...],
                     preferred_element_type=F32)
    ple = (jax.nn.sigmoid(jnp.dot(hb, wpg_ref[...], preferred_element_type=F32))
           * jnp.dot(p_ref[...].astype(BF16), wpp_ref[...], preferred_element_type=F32))
    t = alpha * h_ref[...] + (r_ref[...] + shared) + ple
    y_ref[...] = _layer_norm(t, lg_ref[...], lb_ref[...])


def _final(h, hb, routed, routed_row0, p2d, w, alpha):
    n, d = h.shape
    tm = min(ROW_TILE, n)
    assert routed_row0 % tm == 0
    r_off = routed_row0 // tm
    pd = p2d.shape[1]
    de2 = w["wsg"].shape[1]
    row = lambda i: (i, 0)
    const = lambda i: (0, 0)
    return pl.pallas_call(
        functools.partial(_final_kernel, alpha=alpha),
        grid=(n // tm,),
        in_specs=[pl.BlockSpec((tm, d), row), pl.BlockSpec((tm, d), row),
                  pl.BlockSpec((tm, d), lambda i: (i + r_off, 0)), pl.BlockSpec((tm, pd), row),
                  pl.BlockSpec((d, de2), const), pl.BlockSpec((de2 // 2, d), const),
                  pl.BlockSpec((d, d), const), pl.BlockSpec((pd, d), const),
                  pl.BlockSpec((1, d), const), pl.BlockSpec((1, d), const)],
        out_specs=pl.BlockSpec((tm, d), row),
        out_shape=jax.ShapeDtypeStruct((n, d), F32),
        compiler_params=_cparams("parallel"), name="final",
    )(h, hb, routed, p2d, w["wsg"], w["wsd"], w["wpg"], w["wpp"], w["ln2_g"], w["ln2_b"])


def _moe_routed(hb, top_e_t, gate_t, w_gu, w_dn):
    n, d = hb.shape
    n_exp = w_gu.shape[0]
    k = top_e_t.shape[0]
    rb = MOE_ROW_BLOCK
    n_assign = n * k
    n_blocks = -(-n_assign // rb)
    rows = n_blocks * rb
    ids = jnp.arange(n_assign, dtype=jnp.int32)
    flat_e = top_e_t.reshape(-1)
    _, sorted_id, sorted_gate = lax.sort((flat_e, ids, gate_t.reshape(-1)), num_keys=1, is_stable=True)
    _, pos = lax.sort((sorted_id, ids), num_keys=1)
    sorted_tok = jnp.pad(sorted_id % n, (0, rows - n_assign))
    sorted_gate = jnp.pad(sorted_gate, (0, rows - n_assign))

    counts = jnp.sum((flat_e[:, None] == jnp.arange(n_exp, dtype=jnp.int32)[None, :]).astype(jnp.int32), axis=0)
    group_end = jnp.cumsum(counts)
    starts = jnp.sort(jnp.concatenate([jnp.arange(n_blocks, dtype=jnp.int32) * rb,
                                       group_end[:n_exp - 1].astype(jnp.int32)]))
    ends = jnp.concatenate([starts[1:], jnp.full((1,), n_assign, jnp.int32)])
    work_blk = jnp.minimum(starts // rb, n_blocks - 1)
    work_exp = jnp.minimum(jnp.sum((group_end[None, :] <= starts[:, None]).astype(jnp.int32), axis=1), n_exp - 1)

    xs = hb[sorted_tok]
    y = _experts((work_blk, work_exp, starts, ends), xs, sorted_gate[:, None], w_gu, w_dn)
    return jnp.sum(y[pos.reshape(k, n)].astype(F32), axis=0)


def _head_token_rows(q2d, n_seq, n_new, n_heads):
    q = q2d.reshape(n_seq, n_new, n_heads, HEAD_DIM)
    return jnp.transpose(q, (0, 2, 1, 3)).reshape(n_seq, n_heads * n_new, HEAD_DIM)


def _token_major(o, n_seq, n_new, n_heads):
    o = o.reshape(n_seq, n_heads, n_new, HEAD_DIM)
    return jnp.transpose(o, (0, 2, 1, 3)).reshape(n_seq * n_new, n_heads * HEAD_DIM)


def _key_head_rows(kv2d, n_seq, n_new, n_heads, n_rows):
    a = kv2d.reshape(n_seq, n_new * n_heads, HEAD_DIM)
    return jnp.pad(a, ((0, 0), (0, n_rows - n_new * n_heads), (0, 0)))


def _prep_weights(i, w_in, w_branch_sb, w_branch_moba, w_out, ln1_g, ln1_b, w_router, b_router,
                  w_sh_gu, w_sh_down, w_ple_gate, w_ple_proj, ln2_g, ln2_b):
    wi = w_in[i]
    d = wi.shape[0]
    s, m = SB_WIDTH, MOBA_WIDTH
    q_sb, k_sb, v_sb = wi[:, 0:s], wi[:, s:2 * s], wi[:, 2 * s:3 * s]
    o = 3 * s
    q_mb, k_mb, v_mb = wi[:, o:o + m], wi[:, o + m:o + 2 * m], wi[:, o + 2 * m:o + 3 * m]
    wv = jnp.concatenate([v_sb, v_mb], axis=1).astype(BF16)
    wr_hi, wr_lo = _split_bf16(w_router[i].T)
    return {
        "wq": jnp.concatenate([q_sb, q_mb], axis=1).astype(BF16),
        "wk": jnp.concatenate([k_sb, k_mb], axis=1).astype(BF16),
        "wv": wv, "wvt": wv.T, "wg": wi[:, o + 3 * m:].astype(BF16),
        "wsb": w_branch_sb[i].astype(BF16), "wmb": w_branch_moba[i].astype(BF16),
        "wout": w_out[i].astype(BF16),
        "ln1_g": ln1_g[i].reshape(1, d), "ln1_b": ln1_b[i].reshape(1, d),
        "wr_hi": wr_hi, "wr_lo": wr_lo, "b_router": b_router[i].reshape(-1, 1).astype(F32),
        "wsg": w_sh_gu[i].astype(BF16), "wsd": w_sh_down[i].astype(BF16),
        "wpg": w_ple_gate[i].astype(BF16), "wpp": w_ple_proj[i].astype(BF16),
        "ln2_g": ln2_g[i].reshape(1, d), "ln2_b": ln2_b[i].reshape(1, d),
    }


def kernel(x_prompt, x_sample, cache_k, cache_v, page_table, p_prompt, p_sample, w_in, w_branch_sb, w_branch_moba, w_out, ln1_g, ln1_b, w_router, b_router, w_exp_gu, w_exp_down, w_sh_gu, w_sh_down, w_ple_gate, w_ple_proj, ln2_g, ln2_b):
    depth = w_in.shape[0]
    alpha = (2 * depth) ** 0.25
    bp, tp, d = x_prompt.shape
    bs, ts, _ = x_sample.shape
    n_p, n_s = bp * tp, bs * ts
    n_pages = page_table.shape[1]
    page = cache_k.shape[2]
    past = n_pages * page
    width = SB_WIDTH + MOBA_WIDTH
    n_heads = SB_HEADS + MOBA_HEADS
    assert SB_HEADS == MOBA_HEADS == SUBLANES and cache_k.shape[3] == n_heads
    assert tp % ATT_BLOCK == 0 and past % MOBA_BLOCK == 0
    new_rows = -(-ts * SB_HEADS // LANES) * LANES
    assert n_p % min(ROW_TILE, n_s) == 0

    tab_p = _rope_tables(jnp.arange(tp))
    tab_s = _rope_tables(past + (jnp.arange(n_s) % ts))

    h_p = x_prompt.reshape(n_p, d)
    h_s = x_sample.reshape(n_s, d)
    k_p_rows, v_p_rows, k_s_rows, v_s_rows = [], [], [], []
    for i in range(depth):
        w = _prep_weights(i, w_in, w_branch_sb, w_branch_moba, w_out, ln1_g, ln1_b, w_router, b_router,
                          w_sh_gu, w_sh_down, w_ple_gate, w_ple_proj, ln2_g, ln2_b)

        kf, vf, qb, kb, g_p, vt, km = _inproj(h_p, w, tab_p, tp, True)
        k_p_rows.append(kf.reshape(bp, tp, n_heads, HEAD_DIM))
        v_p_rows.append(vf.reshape(bp, tp, n_heads, HEAD_DIM))
        qb3, kb3 = qb.reshape(bp, tp, width), kb.reshape(bp, tp, width)
        osb_p = _sb_prompt(qb3, kb3, vt).reshape(n_p, SB_WIDTH)
        omb_p = _moba_prompt(qb3, kb3, vt, km.reshape(bp, tp // MOBA_BLOCK, MOBA_WIDTH)).reshape(n_p, MOBA_WIDTH)
        h1_p, h1b_p = _post_attn(osb_p, omb_p, g_p, h_p, w, alpha)

        kf_s, vf_s, qb_s, g_s = _inproj(h_s, w, tab_s, ts, False)
        k_s_rows.append(kf_s.reshape(bs, ts, n_heads, HEAD_DIM))
        v_s_rows.append(vf_s.reshape(bs, ts, n_heads, HEAD_DIM))
        o_sb = _sb_sample(i, page_table, _head_token_rows(qb_s[:, :SB_WIDTH], bs, ts, SB_HEADS),
                          _key_head_rows(kf_s[:, :SB_WIDTH], bs, ts, SB_HEADS, new_rows),
                          _key_head_rows(vf_s[:, :SB_WIDTH], bs, ts, SB_HEADS, new_rows),
                          cache_k, cache_v, ts)
        o_mb = _moba_sample(i, page_table, _head_token_rows(qb_s[:, SB_WIDTH:], bs, ts, MOBA_HEADS),
                            _key_head_rows(kf_s[:, SB_WIDTH:], bs, ts, MOBA_HEADS, new_rows),
                            _key_head_rows(vf_s[:, SB_WIDTH:], bs, ts, MOBA_HEADS, new_rows),
                            cache_k, cache_v, ts)
        osb_s = _token_major(o_sb, bs, ts, SB_HEADS).astype(BF16)
        omb_s = _token_major(o_mb, bs, ts, MOBA_HEADS).astype(BF16)
        h1_s, h1b_s = _post_attn(osb_s, omb_s, g_s, h_s, w, alpha)

        e_p, g_rp = _router(h1_p, w)
        e_s, g_rs = _router(h1_s, w)
        routed = _moe_routed(jnp.concatenate([h1b_p, h1b_s], axis=0),
                             jnp.concatenate([e_p, e_s], axis=1), jnp.concatenate([g_rp, g_rs], axis=1),
                             w_exp_gu[i].astype(BF16), w_exp_down[i].astype(BF16))

        h_p = _final(h1_p, h1b_p, routed, 0, p_prompt[i].reshape(n_p, -1), w, alpha)
        h_s = _final(h1_s, h1b_s, routed, n_p, p_sample[i].reshape(n_s, -1), w, alpha)

    return (h_p.reshape(bp, tp, d), h_s.reshape(bs, ts, d),
            jnp.stack(k_p_rows), jnp.stack(v_p_rows), jnp.stack(k_s_rows), jnp.stack(v_s_rows))
```

```python
import functools
import math

import jax
import jax.numpy as jnp
from jax import lax
from jax.experimental import pallas as pl
from jax.experimental.pallas import tpu as pltpu

F32 = jnp.float32
BF16 = jnp.bfloat16
U32 = jnp.uint32

HEAD_DIM = 64
SB_HEADS = 8
MOBA_HEADS = 8
SB_WIDTH = SB_HEADS * HEAD_DIM
MOBA_WIDTH = MOBA_HEADS * HEAD_DIM
ROPE_THETA = 10000.0
MOBA_BLOCK = 256
MOBA_TOPK = 3
EXPERT_TOPK = 8
N_EXPERT_GROUPS = 8
TOPK_GROUPS = 4
ROUTED_SCALE = 2.5
LN_EPS = 1e-5

LANES = 128
SUBLANES = 8
HEADS_PER_LANE_BLOCK = LANES // HEAD_DIM
ROW_TILE = 512
ATT_BLOCK = 256
MOE_ROW_BLOCK = 256
VMEM_LIMIT = 56 * 1024 * 1024
NEG_BIG = -1e30
SB_EXIT = -104.0

_NT = (((1,), (1,)), ((), ()))


def _cparams(*sem):
    return pltpu.CompilerParams(dimension_semantics=sem, vmem_limit_bytes=VMEM_LIMIT)


def _split_bf16(x):
    hi = x.astype(BF16)
    lo = (x - hi.astype(F32)).astype(BF16)
    return hi, lo


def _head_mask(shape, h, lane_axis):
    lane = lax.broadcasted_iota(jnp.int32, shape, lane_axis)
    return (lane >= h * HEAD_DIM) & (lane < (h + 1) * HEAD_DIM)


def _log_sigmoid_neg(z):
    return -(jnp.maximum(z, 0.0) + jnp.log(1.0 + jnp.exp(-jnp.abs(z))))


def _pack_pairs(x):
    k = x.shape[1] // 2
    lo = lax.bitcast_convert_type(x[:, :k].astype(BF16).astype(F32), U32)
    hi = lax.bitcast_convert_type(x[:, k:].astype(BF16).astype(F32), U32)
    return hi | (lo >> 16)


def _unpack_pairs(u):
    lo = lax.bitcast_convert_type(u << 16, F32)
    hi = lax.bitcast_convert_type(u & jnp.uint32(0xFFFF0000), F32)
    return lo, hi


def _rope_lanes(y, cos, s1, s2):
    outs = []
    for c in range(y.shape[1] // LANES):
        yc = y[:, c * LANES:(c + 1) * LANES]
        outs.append(yc * cos + pltpu.roll(yc, LANES - HEAD_DIM // 2, 1) * s1
                    + pltpu.roll(yc, HEAD_DIM // 2, 1) * s2)
    return jnp.concatenate(outs, axis=1)


def _inproj_kernel(x_ref, wq_ref, wk_ref, wv_ref, wg_ref, cos_ref, s1_ref, s2_ref, *refs, prompt):
    if prompt:
        wvt_ref, kf_ref, vf_ref, qb_ref, kb_ref, g_ref, vt_ref, km_ref = refs
    else:
        kf_ref, vf_ref, qb_ref, g_ref = refs
    xb = x_ref[...].astype(BF16)
    cos, s1, s2 = cos_ref[...], s1_ref[...], s2_ref[...]
    scale = 1.0 / math.sqrt(HEAD_DIM)

    yq = jnp.dot(xb, wq_ref[...], preferred_element_type=F32)
    qb_ref[:, :SB_WIDTH] = (yq[:, :SB_WIDTH] * scale).astype(BF16)
    qb_ref[:, SB_WIDTH:] = (_rope_lanes(yq[:, SB_WIDTH:], cos, s1, s2) * scale).astype(BF16)

    yk = jnp.dot(xb, wk_ref[...], preferred_element_type=F32)
    k_mb = _rope_lanes(yk[:, SB_WIDTH:], cos, s1, s2)
    kf_ref[:, :SB_WIDTH] = yk[:, :SB_WIDTH]
    kf_ref[:, SB_WIDTH:] = k_mb

    vf_ref[...] = jnp.dot(xb, wv_ref[...], preferred_element_type=F32)
    g_ref[...] = jnp.dot(xb, wg_ref[...], preferred_element_type=F32).astype(BF16)

    if prompt:
        kb_ref[:, :SB_WIDTH] = yk[:, :SB_WIDTH].astype(BF16)
        kb_ref[:, SB_WIDTH:] = k_mb.astype(BF16)
        rows = xb.shape[0]
        for r in range(rows // MOBA_BLOCK):
            blk = k_mb[r * MOBA_BLOCK:(r + 1) * MOBA_BLOCK]
            km_ref[0, r:r + 1, :] = jnp.sum(blk, axis=0, keepdims=True) * (1.0 / MOBA_BLOCK)
        vt = lax.dot_general(wvt_ref[...], xb, _NT, preferred_element_type=F32)
        for r in range(rows // ATT_BLOCK):
            vt_ref[0, r] = vt[:, r * ATT_BLOCK:(r + 1) * ATT_BLOCK].astype(BF16)


def _rope_tables(positions):
    half = HEAD_DIM // 2
    inv_freq = jnp.power(ROPE_THETA, -jnp.arange(half, dtype=F32) / half)
    ang = positions.astype(F32)[:, None] * inv_freq[None, :]
    cos, sin = jnp.cos(ang), jnp.sin(ang)
    zero = jnp.zeros_like(sin)
    reps = LANES // HEAD_DIM
    cos_t = jnp.tile(jnp.concatenate([cos, cos], axis=1), (1, reps))
    s1_t = jnp.tile(jnp.concatenate([-sin, zero], axis=1), (1, reps))
    s2_t = jnp.tile(jnp.concatenate([zero, sin], axis=1), (1, reps))
    return cos_t, s1_t, s2_t


def _inproj(x2d, w, tables, seq_len, prompt):
    n, d = x2d.shape
    tm = min(ROW_TILE, n)
    assert n % tm == 0
    width = SB_WIDTH + MOBA_WIDTH
    n_tab = tables[0].shape[0] // tm
    row = lambda i: (i, 0)
    const = lambda i: (0, 0)
    tab = lambda i: (i % n_tab, 0)
    in_specs = [pl.BlockSpec((tm, d), row),
                pl.BlockSpec((d, width), const), pl.BlockSpec((d, width), const),
                pl.BlockSpec((d, width), const), pl.BlockSpec((d, 2 * d), const),
                pl.BlockSpec((tm, LANES), tab), pl.BlockSpec((tm, LANES), tab),
                pl.BlockSpec((tm, LANES), tab)]
    args = [x2d, w["wq"], w["wk"], w["wv"], w["wg"], *tables]
    out_shape = [jax.ShapeDtypeStruct((n, width), F32), jax.ShapeDtypeStruct((n, width), F32),
                 jax.ShapeDtypeStruct((n, width), BF16)]
    out_specs = [pl.BlockSpec((tm, width), row)] * 3
    if prompt:
        out_shape.append(jax.ShapeDtypeStruct((n, width), BF16))
        out_specs.append(pl.BlockSpec((tm, width), row))
    out_shape.append(jax.ShapeDtypeStruct((n, 2 * d), BF16))
    out_specs.append(pl.BlockSpec((tm, 2 * d), row))
    if prompt:
        assert tm % ATT_BLOCK == 0 and seq_len % tm == 0
        per_seq = seq_len // tm
        in_specs.append(pl.BlockSpec((width, d), const))
        args.append(w["wvt"])
        out_shape += [jax.ShapeDtypeStruct((n // seq_len, seq_len // ATT_BLOCK, width, ATT_BLOCK), BF16),
                      jax.ShapeDtypeStruct((n // tm, tm // MOBA_BLOCK, MOBA_WIDTH), F32)]
        out_specs += [pl.BlockSpec((1, tm // ATT_BLOCK, width, ATT_BLOCK),
                                   lambda i: (i // per_seq, i % per_seq, 0, 0)),
                      pl.BlockSpec((1, tm // MOBA_BLOCK, MOBA_WIDTH), lambda i: (i, 0, 0))]
    return pl.pallas_call(
        functools.partial(_inproj_kernel, prompt=prompt),
        grid=(n // tm,), in_specs=in_specs, out_specs=out_specs, out_shape=out_shape,
        compiler_params=_cparams("parallel"), name="inproj_prompt" if prompt else "inproj_sample",
    )(*args)


def _strict_upper(n):
    r = lax.broadcasted_iota(jnp.int32, (n, n), 0)
    c = lax.broadcasted_iota(jnp.int32, (n, n), 1)
    return r < c


def _sb_block_t(z_t, carry, acc, vt_blk, strict, upper):
    log_keep = _log_sigmoid_neg(z_t)
    if strict is not None:
        log_keep = jnp.where(strict, log_keep, 0.0)
    hi, lo = _split_bf16(log_keep)
    after = (jnp.dot(upper, hi, preferred_element_type=F32)
             + jnp.dot(upper, lo, preferred_element_type=F32))
    w = jnp.exp(z_t + log_keep + after + carry)
    if strict is not None:
        w = jnp.where(strict, w, 0.0)
    acc = acc + jnp.dot(vt_blk, w.astype(BF16), preferred_element_type=F32)
    carry = carry + jnp.sum(log_keep, axis=0, keepdims=True)
    return carry, acc


def _sb_prompt_kernel(q_ref, k_ref, vt_ref, o_ref):
    i = pl.program_id(2)
    tb = q_ref.shape[1]
    q = q_ref[0]
    strict = _strict_upper(tb)
    upper = jnp.where(strict, 1.0, 0.0).astype(BF16)
    qh = [jnp.where(_head_mask(q.shape, h, 1), q, jnp.zeros_like(q)) for h in range(HEADS_PER_LANE_BLOCK)]

    def visit(j, carries, accs, masked):
        kb = k_ref[0, pl.ds(pl.multiple_of(j * tb, tb), tb), :]
        vtb = vt_ref[0, j]
        outs = [_sb_block_t(lax.dot_general(kb, qh[h], _NT, preferred_element_type=F32),
                            carries[h], accs[h], vtb, strict if masked else None, upper)
                for h in range(HEADS_PER_LANE_BLOCK)]
        return tuple(o[0] for o in outs), tuple(o[1] for o in outs)

    zc = jnp.zeros((1, tb), F32)
    za = jnp.zeros((LANES, tb), F32)
    carries, accs = visit(i, (zc,) * HEADS_PER_LANE_BLOCK, (za,) * HEADS_PER_LANE_BLOCK, True)

    def cond(st):
        jj, carries, _ = st
        live = functools.reduce(jnp.maximum, [jnp.max(c) for c in carries]) > SB_EXIT
        return jnp.logical_and(jj < i, live)

    def body(st):
        jj, carries, accs = st
        carries, accs = visit(i - 1 - jj, carries, accs, False)
        return jj + 1, carries, accs

    _, carries, accs = lax.while_loop(cond, body, (jnp.int32(0), carries, accs))
    rows = lax.broadcasted_iota(jnp.int32, (LANES, tb), 0)
    acc_t = jnp.where(rows < HEAD_DIM, accs[0], accs[1])
    o_ref[0] = acc_t.T.astype(o_ref.dtype)


def _sb_prompt(qb, kb, vt):
    b, t, _ = qb.shape
    tb = ATT_BLOCK
    n_pairs = SB_WIDTH // LANES
    return pl.pallas_call(
        _sb_prompt_kernel,
        grid=(b, n_pairs, t // tb),
        in_specs=[pl.BlockSpec((1, tb, LANES), lambda bi, hp, i: (bi, i, hp)),
                  pl.BlockSpec((1, t, LANES), lambda bi, hp, i: (bi, 0, hp)),
                  pl.BlockSpec((1, t // tb, LANES, tb), lambda bi, hp, i: (bi, 0, hp, 0))],
        out_specs=pl.BlockSpec((1, tb, LANES), lambda bi, hp, i: (bi, i, hp)),
        out_shape=jax.ShapeDtypeStruct((b, t, SB_WIDTH), BF16),
        compiler_params=_cparams("parallel", "parallel", "parallel"), name="sb_prompt",
    )(qb, kb, vt)


def _fold_heads(o_full, n_new):
    row_head = lax.broadcasted_iota(jnp.int32, (o_full.shape[0], HEAD_DIM), 0) // n_new
    out = jnp.zeros((o_full.shape[0], HEAD_DIM), F32)
    for h in range(o_full.shape[1] // HEAD_DIM):
        out = jnp.where(row_head == h, o_full[:, h * HEAD_DIM:(h + 1) * HEAD_DIM], out)
    return out


def _sb_keys(q, kt, vt, valid, carry, acc, lower):
    s = jnp.dot(q, kt, preferred_element_type=F32)
    log_keep = _log_sigmoid_neg(s)
    if valid is not None:
        log_keep = jnp.where(valid, log_keep, 0.0)
    hi, lo = _split_bf16(log_keep)
    after = (jnp.dot(hi, lower, preferred_element_type=F32)
             + jnp.dot(lo, lower, preferred_element_type=F32))
    w = jnp.exp(s + log_keep + after + carry)
    if valid is not None:
        w = jnp.where(valid, w, 0.0)
    acc = acc + lax.dot_general(w.astype(BF16), vt, _NT, preferred_element_type=F32)
    carry = carry + jnp.sum(log_keep, axis=1, keepdims=True)
    return carry, acc


def _later_lane(n):
    r = lax.broadcasted_iota(jnp.int32, (n, n), 0)
    c = lax.broadcasted_iota(jnp.int32, (n, n), 1)
    return jnp.where(r > c, 1.0, 0.0).astype(BF16)


def _sb_sample_kernel(pt_ref, q_ref, kn_ref, vn_ref, kc_hbm, vc_hbm, o_ref, kbuf, vbuf, sem,
                      *, layer, n_new, n_pages):
    b = pl.program_id(0)
    q = q_ref[0]
    rows, width = q.shape
    _, heads, _, page = kbuf.shape

    def copies(p, slot):
        pid = pt_ref[b, p]
        src = lambda ref: ref.at[layer, pid, pl.ds(0, heads), pl.ds(0, HEAD_DIM), pl.ds(0, page)]
        return (pltpu.make_async_copy(src(kc_hbm), kbuf.at[slot], sem.at[0, slot]),
                pltpu.make_async_copy(src(vc_hbm), vbuf.at[slot], sem.at[1, slot]))

    def start(p, slot):
        for c in copies(p, slot):
            c.start()

    def wait(p, slot):
        for c in copies(p, slot):
            c.wait()

    start(n_pages - 1, 0)

    n = kn_ref.shape[2]
    key = lax.broadcasted_iota(jnp.int32, (rows, n), 1)
    tok = lax.broadcasted_iota(jnp.int32, (rows, n), 0) % n_new
    carry, acc = _sb_keys(q, kn_ref[0].astype(BF16), vn_ref[0].astype(BF16), key < tok,
                          jnp.zeros((rows, 1), F32), jnp.zeros((rows, width), F32), _later_lane(n))
    lower = _later_lane(page)

    def cond(st):
        k, carry, _ = st
        return jnp.logical_and(k < n_pages, jnp.max(carry) > SB_EXIT)

    def body(st):
        k, carry, acc = st
        p = n_pages - 1 - k
        slot = k % 2
        wait(p, slot)

        @pl.when(k + 1 < n_pages)
        def _():
            start(p - 1, 1 - slot)

        kt = kbuf[slot].reshape(width, page).astype(BF16)
        vt = vbuf[slot].reshape(width, page).astype(BF16)
        carry, acc = _sb_keys(q, kt, vt, None, carry, acc, lower)
        return k + 1, carry, acc

    k, carry, acc = lax.while_loop(cond, body, (jnp.int32(0), carry, acc))

    @pl.when(k < n_pages)
    def _():
        wait(n_pages - 1 - k, k % 2)

    o_ref[0] = _fold_heads(acc, n_new)


def _sb_sample(layer, page_table, q_rows, kt_new, vt_new, cache_kt, cache_vt, n_new):
    bs, rows, width = q_rows.shape
    n_pages = page_table.shape[1]
    page = cache_kt.shape[4]
    new_lanes = kt_new.shape[2]
    grid_spec = pltpu.PrefetchScalarGridSpec(
        num_scalar_prefetch=1, grid=(bs,),
        in_specs=[pl.BlockSpec((1, rows, width), lambda b, pt: (b, 0, 0)),
                  pl.BlockSpec((1, width, new_lanes), lambda b, pt: (b, 0, 0)),
                  pl.BlockSpec((1, width, new_lanes), lambda b, pt: (b, 0, 0)),
                  pl.BlockSpec(memory_space=pl.ANY), pl.BlockSpec(memory_space=pl.ANY)],
        out_specs=pl.BlockSpec((1, rows, HEAD_DIM), lambda b, pt: (b, 0, 0)),
        scratch_shapes=[pltpu.VMEM((2, SB_HEADS, HEAD_DIM, page), F32),
                        pltpu.VMEM((2, SB_HEADS, HEAD_DIM, page), F32),
                        pltpu.SemaphoreType.DMA((2, 2))])
    return pl.pallas_call(
        functools.partial(_sb_sample_kernel, layer=layer, n_new=n_new, n_pages=n_pages),
        grid_spec=grid_spec, out_shape=jax.ShapeDtypeStruct((bs, rows, HEAD_DIM), F32),
        compiler_params=_cparams("arbitrary"), name="sb_sample",
    )(page_table, q_rows, kt_new, vt_new, cache_kt, cache_vt)


def _top_rows(gate, valid, n_sel, axis):
    idx = lax.broadcasted_iota(jnp.int32, gate.shape, axis)
    size = gate.shape[axis]
    cur = jnp.where(valid, gate, -jnp.inf)
    sel = jnp.zeros(gate.shape, jnp.bool_)
    for _ in range(n_sel):
        m = jnp.max(cur, axis=axis, keepdims=True)
        first = jnp.min(jnp.where(cur == m, idx, size), axis=axis, keepdims=True)
        pick = (idx == first) & valid
        sel = sel | pick
        cur = jnp.where(pick, -jnp.inf, cur)
    return sel


def _moba_prompt_kernel(q_ref, k_ref, vt_ref, km_ref, o_ref, sel_ref):
    i = pl.program_id(2)
    tb = q_ref.shape[1]
    nb = km_ref.shape[1]
    nh = HEADS_PER_LANE_BLOCK
    q = q_ref[0]
    km_hi, km_lo = _split_bf16(km_ref[0])
    key = lax.broadcasted_iota(jnp.int32, (tb, tb), 0)
    qry = lax.broadcasted_iota(jnp.int32, (tb, tb), 1)
    causal = key <= qry
    blk = lax.broadcasted_iota(jnp.int32, (nb, tb), 0)
    qh = [jnp.where(_head_mask(q.shape, h, 1), q, jnp.zeros_like(q)) for h in range(nh)]
    for h in range(nh):
        gate = (lax.dot_general(km_hi, qh[h], _NT, preferred_element_type=F32)
                + lax.dot_general(km_lo, qh[h], _NT, preferred_element_type=F32))
        sel_ref[h, 0:nb, :] = jnp.where(_top_rows(gate, blk < i, MOBA_TOPK, 0), 1.0, 0.0)

    def keys(j):
        return k_ref[0, pl.ds(pl.multiple_of(j * tb, tb), tb), :]

    kb, vtb = keys(i), vt_ref[0, i]
    state = []
    for h in range(nh):
        s_t = jnp.where(causal, lax.dot_general(kb, qh[h], _NT, preferred_element_type=F32), -jnp.inf)
        m = jnp.max(s_t, axis=0, keepdims=True)
        p = jnp.exp(s_t - m)
        state += [m, jnp.sum(p, axis=0, keepdims=True),
                  jnp.dot(vtb, p.astype(BF16), preferred_element_type=F32)]

    def step(j, state):
        kb, vtb = keys(j), vt_ref[0, j]
        out = []
        for h in range(nh):
            m, l, acc = state[3 * h:3 * h + 3]
            picked = sel_ref[h, pl.ds(j, 1), :] > 0.5
            s_t = jnp.where(picked, lax.dot_general(kb, qh[h], _NT, preferred_element_type=F32), -jnp.inf)
            m_new = jnp.maximum(m, jnp.max(s_t, axis=0, keepdims=True))
            alpha = jnp.exp(m - m_new)
            p = jnp.exp(s_t - m_new)
            out += [m_new, alpha * l + jnp.sum(p, axis=0, keepdims=True),
                    alpha * acc + jnp.dot(vtb, p.astype(BF16), preferred_element_type=F32)]
        return tuple(out)

    state = lax.fori_loop(0, i, step, tuple(state))
    rows = lax.broadcasted_iota(jnp.int32, (LANES, tb), 0)
    out_t = jnp.where(rows < HEAD_DIM, state[2] / state[1], state[5] / state[4])
    o_ref[0] = out_t.T.astype(o_ref.dtype)


def _moba_prompt(qb, kb, vt, kmean):
    b, t, _ = qb.shape
    tb = ATT_BLOCK
    nb = t // MOBA_BLOCK
    n_pairs = MOBA_WIDTH // LANES
    off = SB_WIDTH // LANES
    return pl.pallas_call(
        _moba_prompt_kernel,
        grid=(b, n_pairs, t // tb),
        in_specs=[pl.BlockSpec((1, tb, LANES), lambda bi, hp, i: (bi, i, off + hp)),
                  pl.BlockSpec((1, t, LANES), lambda bi, hp, i: (bi, 0, off + hp)),
                  pl.BlockSpec((1, t // tb, LANES, tb), lambda bi, hp, i: (bi, 0, off + hp, 0)),
                  pl.BlockSpec((1, nb, LANES), lambda bi, hp, i: (bi, 0, hp))],
        out_specs=pl.BlockSpec((1, tb, LANES), lambda bi, hp, i: (bi, i, hp)),
        out_shape=jax.ShapeDtypeStruct((b, t, MOBA_WIDTH), BF16),
        scratch_shapes=[pltpu.VMEM((HEADS_PER_LANE_BLOCK, -(-nb // SUBLANES) * SUBLANES, tb), F32)],
        compiler_params=_cparams("parallel", "parallel", "parallel"), name="moba_prompt",
    )(qb, kb, vt, kmean)


def _moba_sample_kernel(pt_ref, q_ref, kn_ref, vn_ref, *refs, n_new, n_seq):
    del pt_ref
    pages = [refs[4 * u:4 * u + 4] for u in range(n_seq)]
    o_ref, m_ref, l_ref, g_ref, acc_ref = refs[4 * n_seq:]
    s = pl.program_id(1)
    nbp = pl.num_programs(1) - 1
    rows, width = q_ref.shape[1:]
    page = pages[0][0].shape[2]
    blk_lane = lax.broadcasted_iota(jnp.int32, m_ref.shape[1:], 1)

    @pl.when(s == 0)
    def _():
        m_ref[...] = jnp.full(m_ref.shape, NEG_BIG, F32)
        l_ref[...] = jnp.zeros_like(l_ref)
        g_ref[...] = jnp.zeros_like(g_ref)

    @pl.when(s < nbp)
    def _():
        for u in range(n_seq):
            ka_ref, kb_ref, va_ref, vb_ref = pages[u]
            q = q_ref[u]
            flat = lambda ref: ref[...].reshape(width, page).astype(BF16)
            kt = jnp.concatenate([flat(ka_ref), flat(kb_ref)], axis=1)
            vt = jnp.concatenate([flat(va_ref), flat(vb_ref)], axis=1)
            sc = jnp.dot(q, kt, preferred_element_type=F32)
            m = jnp.max(sc, axis=1, keepdims=True)
            p = jnp.exp(sc - m)
            acc_ref[u, s] = lax.dot_general(p.astype(BF16), vt, _NT, preferred_element_type=F32)
            m_ref[u] = jnp.where(blk_lane == s, m, m_ref[u])
            l_ref[u] = jnp.where(blk_lane == s, jnp.sum(p, axis=1, keepdims=True), l_ref[u])
            g_ref[u] = jnp.where(blk_lane == s, jnp.sum(sc, axis=1, keepdims=True), g_ref[u])

    @pl.when(s == nbp)
    def _():
        for u in range(n_seq):
            q = q_ref[u]
            sel = _top_rows(g_ref[u], blk_lane < nbp, MOBA_TOPK, 1)
            s_own = jnp.dot(q, kn_ref[u].astype(BF16), preferred_element_type=F32)
            key = lax.broadcasted_iota(jnp.int32, s_own.shape, 1)
            tok = lax.broadcasted_iota(jnp.int32, s_own.shape, 0) % n_new
            s_own = jnp.where(key <= tok, s_own, -jnp.inf)
            m_all, l_all = m_ref[u], l_ref[u]
            m_tot = jnp.maximum(jnp.max(s_own, axis=1, keepdims=True),
                                jnp.max(jnp.where(sel, m_all, -jnp.inf), axis=1, keepdims=True))
            a = jnp.where(sel, jnp.exp(m_all - m_tot), 0.0)
            p_own = jnp.exp(s_own - m_tot)
            denom = jnp.sum(a * l_all, axis=1, keepdims=True) + jnp.sum(p_own, axis=1, keepdims=True)
            num = lax.dot_general(p_own.astype(BF16), vn_ref[u].astype(BF16), _NT, preferred_element_type=F32)
            for j in range(acc_ref.shape[1]):
                num = num + a[:, j:j + 1] * acc_ref[u, j]
            o_ref[u] = _fold_heads(num, n_new) / denom


def _moba_sample(layer, page_table, q_rows, kt_new, vt_new, cache_kt, cache_vt, n_new):
    bs, rows, width = q_rows.shape
    n_pages = page_table.shape[1]
    page = cache_kt.shape[4]
    new_lanes = kt_new.shape[2]
    per_blk = MOBA_BLOCK // page
    assert per_blk == 2 and n_pages % per_blk == 0
    nbp = n_pages // per_blk
    assert nbp <= LANES
    n_seq = 2 if bs % 2 == 0 else 1
    head_blk = SB_HEADS // MOBA_HEADS

    def page_spec(u, which):
        return pl.BlockSpec(
            (None, None, MOBA_HEADS, HEAD_DIM, page),
            lambda b, s, pt: (layer, pt[b * n_seq + u, per_blk * jnp.minimum(s, nbp - 1) + which], head_blk, 0, 0))

    seq = lambda b, s, pt: (b, 0, 0)
    page_specs, page_args = [], []
    for u in range(n_seq):
        page_specs += [page_spec(u, 0), page_spec(u, 1), page_spec(u, 0), page_spec(u, 1)]
        page_args += [cache_kt, cache_kt, cache_vt, cache_vt]
    grid_spec = pltpu.PrefetchScalarGridSpec(
        num_scalar_prefetch=1, grid=(bs // n_seq, nbp + 1),
        in_specs=[pl.BlockSpec((n_seq, rows, width), seq),
                  pl.BlockSpec((n_seq, width, new_lanes), seq),
                  pl.BlockSpec((n_seq, width, new_lanes), seq)] + page_specs,
        out_specs=pl.BlockSpec((n_seq, rows, HEAD_DIM), seq),
        scratch_shapes=[pltpu.VMEM((n_seq, rows, LANES), F32), pltpu.VMEM((n_seq, rows, LANES), F32),
                        pltpu.VMEM((n_seq, rows, LANES), F32),
                        pltpu.VMEM((n_seq, nbp, rows, width), F32)])
    return pl.pallas_call(
        functools.partial(_moba_sample_kernel, n_new=n_new, n_seq=n_seq),
        grid_spec=grid_spec, out_shape=jax.ShapeDtypeStruct((bs, rows, HEAD_DIM), F32),
        compiler_params=_cparams("parallel", "arbitrary"), name="moba_sample",
    )(page_table, q_rows, kt_new, vt_new, *page_args)


def _layer_norm(t, g, b):
    mu = jnp.mean(t, axis=-1, keepdims=True)
    tc = t - mu
    var = jnp.mean(tc * tc, axis=-1, keepdims=True)
    return tc * lax.rsqrt(var + LN_EPS) * g + b


def _post_attn_kernel(osb_ref, omb_ref, g_ref, x_ref, wsb_ref, wmb_ref, wout_ref, lg_ref, lb_ref,
                      h_ref, hb_ref, hp_ref, *, alpha):
    d = x_ref.shape[1]
    bsb = jnp.dot(osb_ref[...], wsb_ref[...], preferred_element_type=F32)
    bmb = jnp.dot(omb_ref[...], wmb_ref[...], preferred_element_type=F32)
    merged = (jax.nn.sigmoid(g_ref[:, :d].astype(F32)) * bsb
              + jax.nn.sigmoid(g_ref[:, d:].astype(F32)) * bmb)
    t = alpha * x_ref[...] + jnp.dot(merged.astype(BF16), wout_ref[...], preferred_element_type=F32)
    h = _layer_norm(t, lg_ref[...], lb_ref[...])
    h_ref[...] = h
    hb_ref[...] = h.astype(BF16)
    hp_ref[...] = _pack_pairs(h)


def _post_attn(osb, omb, g, x2d, w, alpha):
    n, d = x2d.shape
    tm = min(ROW_TILE, n)
    row = lambda i: (i, 0)
    const = lambda i: (0, 0)
    return pl.pallas_call(
        functools.partial(_post_attn_kernel, alpha=alpha),
        grid=(n // tm,),
        in_specs=[pl.BlockSpec((tm, SB_WIDTH), row), pl.BlockSpec((tm, MOBA_WIDTH), row),
                  pl.BlockSpec((tm, 2 * d), row), pl.BlockSpec((tm, d), row),
                  pl.BlockSpec((SB_WIDTH, d), const), pl.BlockSpec((MOBA_WIDTH, d), const),
                  pl.BlockSpec((d, d), const), pl.BlockSpec((1, d), const), pl.BlockSpec((1, d), const)],
        out_specs=[pl.BlockSpec((tm, d), row), pl.BlockSpec((tm, d), row), pl.BlockSpec((tm, d // 2), row)],
        out_shape=[jax.ShapeDtypeStruct((n, d), F32), jax.ShapeDtypeStruct((n, d), BF16),
                   jax.ShapeDtypeStruct((n, d // 2), U32)],
        compiler_params=_cparams("parallel"), name="post_attn",
    )(osb, omb, g, x2d, w["wsb"], w["wmb"], w["wout"], w["ln1_g"], w["ln1_b"])


def _router_kernel(h_ref, wh_ref, wl_ref, b_ref, e_ref, g_ref):
    hh, hl = _split_bf16(h_ref[...])
    logits = (lax.dot_general(wh_ref[...], hh, _NT, preferred_element_type=F32)
              + lax.dot_general(wh_ref[...], hl, _NT, preferred_element_type=F32)
              + lax.dot_general(wl_ref[...], hh, _NT, preferred_element_type=F32))
    scores = jax.nn.sigmoid(logits)
    biased = scores + b_ref[...]
    n_exp, _ = scores.shape
    per_group = n_exp // N_EXPERT_GROUPS
    group_scores = []
    for g in range(N_EXPERT_GROUPS):
        v = biased[g * per_group:(g + 1) * per_group]
        top2 = _top_rows(v, jnp.ones(v.shape, jnp.bool_), 2, 0)
        group_scores.append(jnp.sum(jnp.where(top2, v, 0.0), axis=0, keepdims=True))
    group_scores = jnp.concatenate(group_scores, axis=0)
    keep = _top_rows(group_scores, jnp.ones(group_scores.shape, jnp.bool_), TOPK_GROUPS, 0)
    masked = jnp.concatenate(
        [jnp.where(keep[g:g + 1, :], biased[g * per_group:(g + 1) * per_group], -jnp.inf)
         for g in range(N_EXPERT_GROUPS)], axis=0)
    row = lax.broadcasted_iota(jnp.int32, masked.shape, 0)
    idxs, gates = [], []
    for _ in range(EXPERT_TOPK):
        m = jnp.max(masked, axis=0, keepdims=True)
        first = jnp.min(jnp.where(masked == m, row, n_exp), axis=0, keepdims=True)
        pick = row == first
        idxs.append(first)
        gates.append(jnp.sum(jnp.where(pick, scores, 0.0), axis=0, keepdims=True))
        masked = jnp.where(pick, -jnp.inf, masked)
    gate = jnp.concatenate(gates, axis=0)
    gate = gate / jnp.sum(gate, axis=0, keepdims=True) * ROUTED_SCALE
    e_ref[...] = jnp.concatenate(idxs, axis=0)
    g_ref[...] = gate


def _router(h, w):
    n, d = h.shape
    tm = min(ROW_TILE, n)
    n_exp = w["wr_hi"].shape[0]
    const = lambda i: (0, 0)
    return pl.pallas_call(
        _router_kernel, grid=(n // tm,),
        in_specs=[pl.BlockSpec((tm, d), lambda i: (i, 0)), pl.BlockSpec((n_exp, d), const),
                  pl.BlockSpec((n_exp, d), const), pl.BlockSpec((n_exp, 1), const)],
        out_specs=[pl.BlockSpec((EXPERT_TOPK, tm), lambda i: (0, i))] * 2,
        out_shape=[jax.ShapeDtypeStruct((EXPERT_TOPK, n), jnp.int32),
                   jax.ShapeDtypeStruct((EXPERT_TOPK, n), F32)],
        compiler_params=_cparams("parallel"), name="router",
    )(h, w["wr_hi"], w["wr_lo"], w["b_router"])


def _experts_kernel(wb_ref, we_ref, ws_ref, wt_ref, x_ref, gate_ref, wgu_ref, wdn_ref, y_ref):
    del we_ref
    w = pl.program_id(0)
    blk = wb_ref[w]
    start, end = ws_ref[w], wt_ref[w]
    rb, half = x_ref.shape

    @pl.when(jnp.logical_or(w == 0, wb_ref[jnp.maximum(w - 1, 0)] != blk))
    def _():
        y_ref[...] = jnp.zeros_like(y_ref)

    @pl.when(end > start)
    def _():
        de = wdn_ref.shape[1]
        x_lo, x_hi = _unpack_pairs(x_ref[...])
        hgu = (jnp.dot(x_lo.astype(BF16), wgu_ref[0, :half, :], preferred_element_type=F32)
               + jnp.dot(x_hi.astype(BF16), wgu_ref[0, half:, :], preferred_element_type=F32))
        act = jax.nn.silu(hgu[:, :de]) * hgu[:, de:]
        y = jnp.dot(act.astype(BF16), wdn_ref[0], preferred_element_type=F32) * gate_ref[...]
        r = blk * rb + lax.broadcasted_iota(jnp.int32, (rb, 1), 0)
        mine = jnp.logical_and(r >= start, r < end)
        y_ref[...] = jnp.where(mine, _pack_pairs(y), y_ref[...])


def _experts(work, xs, row_gate, w_gu, w_dn):
    rows, half = xs.shape
    rb = MOE_ROW_BLOCK
    d, de2 = w_gu.shape[1:]
    n_work = work[0].shape[0]
    grid_spec = pltpu.PrefetchScalarGridSpec(
        num_scalar_prefetch=4, grid=(n_work,),
        in_specs=[pl.BlockSpec((rb, half), lambda w, wb, we, ws, wt: (wb[w], 0)),
                  pl.BlockSpec((rb, 1), lambda w, wb, we, ws, wt: (wb[w], 0)),
                  pl.BlockSpec((1, d, de2), lambda w, wb, we, ws, wt: (we[w], 0, 0)),
                  pl.BlockSpec((1, de2 // 2, d), lambda w, wb, we, ws, wt: (we[w], 0, 0))],
        out_specs=pl.BlockSpec((rb, half), lambda w, wb, we, ws, wt: (wb[w], 0)))
    return pl.pallas_call(
        _experts_kernel, grid_spec=grid_spec, out_shape=jax.ShapeDtypeStruct((rows, half), U32),
        compiler_params=_cparams("arbitrary"), name="experts",
    )(*work, xs, row_gate, w_gu, w_dn)


def _final_kernel(h_ref, hb_ref, r_ref, p_ref, wsg_ref, wsd_ref, wpg_ref, wpp_ref, lg_ref, lb_ref,
                  y_ref, *, alpha):
    hb = hb_ref[...]
    de = wsd_ref.shape[0]
    sgu = jnp.dot(hb, wsg_ref[...], preferred_element_type=F32)
    shared = jnp.dot((jax.nn.silu(sgu[:, :de]) * sgu[:, de:]).astype(BF16), wsd_ref[...],
                     preferred_element_type=F32)
    ple = (jax.nn.sigmoid(jnp.dot(hb, wpg_ref[...], preferred_element_type=F32))
           * jnp.dot(p_ref[...].astype(BF16), wpp_ref[...], preferred_element_type=F32))
    parts = [_unpack_pairs(r_ref[k]) for k in range(r_ref.shape[0])]
    routed = jnp.concatenate([functools.reduce(jnp.add, [p[0] for p in parts]),
                              functools.reduce(jnp.add, [p[1] for p in parts])], axis=1)
    t = alpha * h_ref[...] + (routed + shared) + ple
    y_ref[...] = _layer_norm(t, lg_ref[...], lb_ref[...])


def _final(h, hb, routed_rows, routed_row0, p2d, w, alpha):
    n, d = h.shape
    tm = min(ROW_TILE, n)
    assert routed_row0 % tm == 0
    r_off = routed_row0 // tm
    k = routed_rows.shape[0]
    pd = p2d.shape[1]
    de2 = w["wsg"].shape[1]
    row = lambda i: (i, 0)
    const = lambda i: (0, 0)
    return pl.pallas_call(
        functools.partial(_final_kernel, alpha=alpha),
        grid=(n // tm,),
        in_specs=[pl.BlockSpec((tm, d), row), pl.BlockSpec((tm, d), row),
                  pl.BlockSpec((k, tm, d // 2), lambda i: (0, i + r_off, 0)), pl.BlockSpec((tm, pd), row),
                  pl.BlockSpec((d, de2), const), pl.BlockSpec((de2 // 2, d), const),
                  pl.BlockSpec((d, d), const), pl.BlockSpec((pd, d), const),
                  pl.BlockSpec((1, d), const), pl.BlockSpec((1, d), const)],
        out_specs=pl.BlockSpec((tm, d), row),
        out_shape=jax.ShapeDtypeStruct((n, d), F32),
        compiler_params=_cparams("parallel"), name="final",
    )(h, hb, routed_rows, p2d, w["wsg"], w["wsd"], w["wpg"], w["wpp"], w["ln2_g"], w["ln2_b"])


def _gather_rows(table, idx):
    return jnp.take(table, idx, axis=0)


def _moe_routed(hp, top_e_t, gate_t, w_gu, w_dn):
    n = hp.shape[0]
    n_exp = w_gu.shape[0]
    k = top_e_t.shape[0]
    rb = MOE_ROW_BLOCK
    n_assign = n * k
    n_blocks = -(-n_assign // rb)
    rows = n_blocks * rb
    ids = jnp.arange(n_assign, dtype=jnp.int32)
    flat_e = top_e_t.reshape(-1)
    _, sorted_id, sorted_gate = lax.sort((flat_e, ids, gate_t.reshape(-1)), num_keys=1, is_stable=True)
    _, pos = lax.sort((sorted_id, ids), num_keys=1)
    sorted_tok = jnp.pad(sorted_id % n, (0, rows - n_assign))
    sorted_gate = jnp.pad(sorted_gate, (0, rows - n_assign))

    counts = jnp.sum((flat_e[:, None] == jnp.arange(n_exp, dtype=jnp.int32)[None, :]).astype(jnp.int32), axis=0)
    group_end = jnp.cumsum(counts)
    starts = jnp.sort(jnp.concatenate([jnp.arange(n_blocks, dtype=jnp.int32) * rb,
                                       group_end[:n_exp - 1].astype(jnp.int32)]))
    ends = jnp.concatenate([starts[1:], jnp.full((1,), n_assign, jnp.int32)])
    work_blk = jnp.minimum(starts // rb, n_blocks - 1)
    work_exp = jnp.minimum(jnp.sum((group_end[None, :] <= starts[:, None]).astype(jnp.int32), axis=1), n_exp - 1)

    xs = _gather_rows(hp, sorted_tok)
    y = _experts((work_blk, work_exp, starts, ends), xs, sorted_gate[:, None], w_gu, w_dn)
    return _gather_rows(y, pos.reshape(k, n))


def _expand_heads(q2d, n_seq, n_new, n_heads):
    q = q2d.reshape(n_seq, n_new, n_heads, HEAD_DIM)
    eye = jnp.eye(n_heads, dtype=q2d.dtype)
    return jnp.einsum("bthd,hg->bhtgd", q, eye).reshape(n_seq, n_heads * n_new, n_heads * HEAD_DIM)


def _token_major(o, n_seq, n_new, n_heads):
    o = o.reshape(n_seq, n_heads, n_new, HEAD_DIM)
    return jnp.transpose(o, (0, 2, 1, 3)).reshape(n_seq * n_new, n_heads * HEAD_DIM)


def _new_keys_t(kv2d, n_seq, n_new, n_lanes):
    a = jnp.transpose(kv2d.reshape(n_seq, n_new, -1), (0, 2, 1))
    return jnp.pad(a, ((0, 0), (0, 0), (0, n_lanes - n_new)))


def _prep_weights(i, w_in, w_branch_sb, w_branch_moba, w_out, ln1_g, ln1_b, w_router, b_router,
                  w_sh_gu, w_sh_down, w_ple_gate, w_ple_proj, ln2_g, ln2_b):
    wi = w_in[i]
    d = wi.shape[0]
    s, m = SB_WIDTH, MOBA_WIDTH
    q_sb, k_sb, v_sb = wi[:, 0:s], wi[:, s:2 * s], wi[:, 2 * s:3 * s]
    o = 3 * s
    q_mb, k_mb, v_mb = wi[:, o:o + m], wi[:, o + m:o + 2 * m], wi[:, o + 2 * m:o + 3 * m]
    wv = jnp.concatenate([v_sb, v_mb], axis=1).astype(BF16)
    wr_hi, wr_lo = _split_bf16(w_router[i].T)
    return {
        "wq": jnp.concatenate([q_sb, q_mb], axis=1).astype(BF16),
        "wk": jnp.concatenate([k_sb, k_mb], axis=1).astype(BF16),
        "wv": wv, "wvt": wv.T, "wg": wi[:, o + 3 * m:].astype(BF16),
        "wsb": w_branch_sb[i].astype(BF16), "wmb": w_branch_moba[i].astype(BF16),
        "wout": w_out[i].astype(BF16),
        "ln1_g": ln1_g[i].reshape(1, d), "ln1_b": ln1_b[i].reshape(1, d),
        "wr_hi": wr_hi, "wr_lo": wr_lo, "b_router": b_router[i].reshape(-1, 1).astype(F32),
        "wsg": w_sh_gu[i].astype(BF16), "wsd": w_sh_down[i].astype(BF16),
        "wpg": w_ple_gate[i].astype(BF16), "wpp": w_ple_proj[i].astype(BF16),
        "ln2_g": ln2_g[i].reshape(1, d), "ln2_b": ln2_b[i].reshape(1, d),
    }


def kernel(x_prompt, x_sample, cache_k, cache_v, page_table, p_prompt, p_sample, w_in, w_branch_sb, w_branch_moba, w_out, ln1_g, ln1_b, w_router, b_router, w_exp_gu, w_exp_down, w_sh_gu, w_sh_down, w_ple_gate, w_ple_proj, ln2_g, ln2_b):
    depth = w_in.shape[0]
    alpha = (2 * depth) ** 0.25
    bp, tp, d = x_prompt.shape
    bs, ts, _ = x_sample.shape
    n_p, n_s = bp * tp, bs * ts
    n_pages = page_table.shape[1]
    page = cache_k.shape[2]
    past = n_pages * page
    width = SB_WIDTH + MOBA_WIDTH
    n_heads = SB_HEADS + MOBA_HEADS
    assert SB_HEADS == MOBA_HEADS and cache_k.shape[3] == n_heads
    assert tp % ATT_BLOCK == 0 and past % MOBA_BLOCK == 0 and ts <= LANES
    assert n_p % min(ROW_TILE, n_s) == 0

    tab_p = _rope_tables(jnp.arange(tp))
    tab_s = _rope_tables(past + (jnp.arange(n_s) % ts))
    cache_kt = jnp.transpose(cache_k, (0, 1, 3, 4, 2))
    cache_vt = jnp.transpose(cache_v, (0, 1, 3, 4, 2))

    h_p = x_prompt.reshape(n_p, d)
    h_s = x_sample.reshape(n_s, d)
    k_p_rows, v_p_rows, k_s_rows, v_s_rows = [], [], [], []
    for i in range(depth):
        w = _prep_weights(i, w_in, w_branch_sb, w_branch_moba, w_out, ln1_g, ln1_b, w_router, b_router,
                          w_sh_gu, w_sh_down, w_ple_gate, w_ple_proj, ln2_g, ln2_b)

        kf, vf, qb, kb, g_p, vt, km = _inproj(h_p, w, tab_p, tp, True)
        k_p_rows.append(kf.reshape(bp, tp, n_heads, HEAD_DIM))
        v_p_rows.append(vf.reshape(bp, tp, n_heads, HEAD_DIM))
        qb3, kb3 = qb.reshape(bp, tp, width), kb.reshape(bp, tp, width)
        osb_p = _sb_prompt(qb3, kb3, vt).reshape(n_p, SB_WIDTH)
        omb_p = _moba_prompt(qb3, kb3, vt, km.reshape(bp, tp // MOBA_BLOCK, MOBA_WIDTH)).reshape(n_p, MOBA_WIDTH)
        h1_p, h1b_p, h1p_p = _post_attn(osb_p, omb_p, g_p, h_p, w, alpha)

        kf_s, vf_s, qb_s, g_s = _inproj(h_s, w, tab_s, ts, False)
        k_s_rows.append(kf_s.reshape(bs, ts, n_heads, HEAD_DIM))
        v_s_rows.append(vf_s.reshape(bs, ts, n_heads, HEAD_DIM))
        o_sb = _sb_sample(i, page_table, _expand_heads(qb_s[:, :SB_WIDTH], bs, ts, SB_HEADS),
                          _new_keys_t(kf_s[:, :SB_WIDTH], bs, ts, LANES),
                          _new_keys_t(vf_s[:, :SB_WIDTH], bs, ts, LANES), cache_kt, cache_vt, ts)
        o_mb = _moba_sample(i, page_table, _expand_heads(qb_s[:, SB_WIDTH:], bs, ts, MOBA_HEADS),
                            _new_keys_t(kf_s[:, SB_WIDTH:], bs, ts, LANES),
                            _new_keys_t(vf_s[:, SB_WIDTH:], bs, ts, LANES), cache_kt, cache_vt, ts)
        osb_s = _token_major(o_sb, bs, ts, SB_HEADS).astype(BF16)
        omb_s = _token_major(o_mb, bs, ts, MOBA_HEADS).astype(BF16)
        h1_s, h1b_s, h1p_s = _post_attn(osb_s, omb_s, g_s, h_s, w, alpha)

        e_p, g_rp = _router(h1_p, w)
        e_s, g_rs = _router(h1_s, w)
        routed_rows = _moe_routed(jnp.concatenate([h1p_p, h1p_s], axis=0),
                                  jnp.concatenate([e_p, e_s], axis=1), jnp.concatenate([g_rp, g_rs], axis=1),
                                  w_exp_gu[i].astype(BF16), w_exp_down[i].astype(BF16))

        h_p = _final(h1_p, h1b_p, routed_rows, 0, p_prompt[i].reshape(n_p, -1), w, alpha)
        h_s = _final(h1_s, h1b_s, routed_rows, n_p, p_sample[i].reshape(n_s, -1), w, alpha)

    return (h_p.reshape(bp, tp, d), h_s.reshape(bs, ts, d),
            jnp.stack(k_p_rows), jnp.stack(v_p_rows), jnp.stack(k_s_rows), jnp.stack(v_s_rows))
```

```python
import functools
import math

import jax
import jax.numpy as jnp
from jax import lax
from jax.experimental import pallas as pl
from jax.experimental.pallas import tpu as pltpu

F32 = jnp.float32
BF16 = jnp.bfloat16
U32 = jnp.uint32

HEAD_DIM = 64
SB_HEADS = 8
MOBA_HEADS = 8
SB_WIDTH = SB_HEADS * HEAD_DIM
MOBA_WIDTH = MOBA_HEADS * HEAD_DIM
ROPE_THETA = 10000.0
MOBA_BLOCK = 256
MOBA_TOPK = 3
EXPERT_TOPK = 8
N_EXPERT_GROUPS = 8
TOPK_GROUPS = 4
ROUTED_SCALE = 2.5
LN_EPS = 1e-5

LANES = 128
SUBLANES = 8
HEADS_PER_LANE_BLOCK = LANES // HEAD_DIM
ROW_TILE = 512
ATT_BLOCK = 256
MOE_ROW_BLOCK = 256
VMEM_LIMIT = 56 * 1024 * 1024
NEG_BIG = -1e30
SB_EXIT = -104.0

_NT = (((1,), (1,)), ((), ()))


def _cparams(*sem):
    return pltpu.CompilerParams(dimension_semantics=sem, vmem_limit_bytes=VMEM_LIMIT)


def _split_bf16(x):
    hi = x.astype(BF16)
    lo = (x - hi.astype(F32)).astype(BF16)
    return hi, lo


def _head_mask(shape, h, lane_axis):
    lane = lax.broadcasted_iota(jnp.int32, shape, lane_axis)
    return (lane >= h * HEAD_DIM) & (lane < (h + 1) * HEAD_DIM)


def _log_sigmoid_neg(z):
    return -(jnp.maximum(z, 0.0) + jnp.log(1.0 + jnp.exp(-jnp.abs(z))))


def _pack_pairs(x):
    k = x.shape[1] // 2
    lo = lax.bitcast_convert_type(x[:, :k].astype(BF16).astype(F32), U32)
    hi = lax.bitcast_convert_type(x[:, k:].astype(BF16).astype(F32), U32)
    return hi | (lo >> 16)


def _unpack_pairs(u):
    lo = lax.bitcast_convert_type(u << 16, F32)
    hi = lax.bitcast_convert_type(u & jnp.uint32(0xFFFF0000), F32)
    return lo, hi


def _rope_lanes(y, cos, s1, s2):
    outs = []
    for c in range(y.shape[1] // LANES):
        yc = y[:, c * LANES:(c + 1) * LANES]
        outs.append(yc * cos + pltpu.roll(yc, LANES - HEAD_DIM // 2, 1) * s1
                    + pltpu.roll(yc, HEAD_DIM // 2, 1) * s2)
    return jnp.concatenate(outs, axis=1)


def _inproj_kernel(x_ref, wq_ref, wk_ref, wv_ref, wg_ref, cos_ref, s1_ref, s2_ref, *refs, prompt):
    if prompt:
        wvt_ref, kf_ref, vf_ref, qb_ref, kb_ref, g_ref, vt_ref, km_ref = refs
    else:
        kf_ref, vf_ref, qb_ref, g_ref = refs
    xb = x_ref[...].astype(BF16)
    cos, s1, s2 = cos_ref[...], s1_ref[...], s2_ref[...]
    scale = 1.0 / math.sqrt(HEAD_DIM)

    yq = jnp.dot(xb, wq_ref[...], preferred_element_type=F32)
    qb_ref[:, :SB_WIDTH] = (yq[:, :SB_WIDTH] * scale).astype(BF16)
    qb_ref[:, SB_WIDTH:] = (_rope_lanes(yq[:, SB_WIDTH:], cos, s1, s2) * scale).astype(BF16)

    yk = jnp.dot(xb, wk_ref[...], preferred_element_type=F32)
    k_mb = _rope_lanes(yk[:, SB_WIDTH:], cos, s1, s2)
    kf_ref[:, :SB_WIDTH] = yk[:, :SB_WIDTH]
    kf_ref[:, SB_WIDTH:] = k_mb

    vf_ref[...] = jnp.dot(xb, wv_ref[...], preferred_element_type=F32)
    g_ref[...] = jnp.dot(xb, wg_ref[...], preferred_element_type=F32).astype(BF16)

    if prompt:
        kb_ref[:, :SB_WIDTH] = yk[:, :SB_WIDTH].astype(BF16)
        kb_ref[:, SB_WIDTH:] = k_mb.astype(BF16)
        rows = xb.shape[0]
        for r in range(rows // MOBA_BLOCK):
            blk = k_mb[r * MOBA_BLOCK:(r + 1) * MOBA_BLOCK]
            km_ref[0, r:r + 1, :] = jnp.sum(blk, axis=0, keepdims=True) * (1.0 / MOBA_BLOCK)
        vt = lax.dot_general(wvt_ref[...], xb, _NT, preferred_element_type=F32)
        for r in range(rows // ATT_BLOCK):
            vt_ref[0, r] = vt[:, r * ATT_BLOCK:(r + 1) * ATT_BLOCK].astype(BF16)


def _rope_tables(positions):
    half = HEAD_DIM // 2
    inv_freq = jnp.power(ROPE_THETA, -jnp.arange(half, dtype=F32) / half)
    ang = positions.astype(F32)[:, None] * inv_freq[None, :]
    cos, sin = jnp.cos(ang), jnp.sin(ang)
    zero = jnp.zeros_like(sin)
    reps = LANES // HEAD_DIM
    cos_t = jnp.tile(jnp.concatenate([cos, cos], axis=1), (1, reps))
    s1_t = jnp.tile(jnp.concatenate([-sin, zero], axis=1), (1, reps))
    s2_t = jnp.tile(jnp.concatenate([zero, sin], axis=1), (1, reps))
    return cos_t, s1_t, s2_t


def _inproj(x2d, w, tables, seq_len, prompt):
    n, d = x2d.shape
    tm = min(ROW_TILE, n)
    assert n % tm == 0
    width = SB_WIDTH + MOBA_WIDTH
    n_tab = tables[0].shape[0] // tm
    row = lambda i: (i, 0)
    const = lambda i: (0, 0)
    tab = lambda i: (i % n_tab, 0)
    in_specs = [pl.BlockSpec((tm, d), row),
                pl.BlockSpec((d, width), const), pl.BlockSpec((d, width), const),
                pl.BlockSpec((d, width), const), pl.BlockSpec((d, 2 * d), const),
                pl.BlockSpec((tm, LANES), tab), pl.BlockSpec((tm, LANES), tab),
                pl.BlockSpec((tm, LANES), tab)]
    args = [x2d, w["wq"], w["wk"], w["wv"], w["wg"], *tables]
    out_shape = [jax.ShapeDtypeStruct((n, width), F32), jax.ShapeDtypeStruct((n, width), F32),
                 jax.ShapeDtypeStruct((n, width), BF16)]
    out_specs = [pl.BlockSpec((tm, width), row)] * 3
    if prompt:
        out_shape.append(jax.ShapeDtypeStruct((n, width), BF16))
        out_specs.append(pl.BlockSpec((tm, width), row))
    out_shape.append(jax.ShapeDtypeStruct((n, 2 * d), BF16))
    out_specs.append(pl.BlockSpec((tm, 2 * d), row))
    if prompt:
        assert tm % ATT_BLOCK == 0 and seq_len % tm == 0
        per_seq = seq_len // tm
        in_specs.append(pl.BlockSpec((width, d), const))
        args.append(w["wvt"])
        out_shape += [jax.ShapeDtypeStruct((n // seq_len, seq_len // ATT_BLOCK, width, ATT_BLOCK), BF16),
                      jax.ShapeDtypeStruct((n // tm, tm // MOBA_BLOCK, MOBA_WIDTH), F32)]
        out_specs += [pl.BlockSpec((1, tm // ATT_BLOCK, width, ATT_BLOCK),
                                   lambda i: (i // per_seq, i % per_seq, 0, 0)),
                      pl.BlockSpec((1, tm // MOBA_BLOCK, MOBA_WIDTH), lambda i: (i, 0, 0))]
    return pl.pallas_call(
        functools.partial(_inproj_kernel, prompt=prompt),
        grid=(n // tm,), in_specs=in_specs, out_specs=out_specs, out_shape=out_shape,
        compiler_params=_cparams("parallel"), name="inproj_prompt" if prompt else "inproj_sample",
    )(*args)


def _strict_upper(n):
    r = lax.broadcasted_iota(jnp.int32, (n, n), 0)
    c = lax.broadcasted_iota(jnp.int32, (n, n), 1)
    return r < c


def _sb_block_t(z_t, carry, acc, vt_blk, strict, upper):
    log_keep = _log_sigmoid_neg(z_t)
    if strict is not None:
        log_keep = jnp.where(strict, log_keep, 0.0)
    hi, lo = _split_bf16(log_keep)
    after = (jnp.dot(upper, hi, preferred_element_type=F32)
             + jnp.dot(upper, lo, preferred_element_type=F32))
    w = jnp.exp(z_t + log_keep + after + carry)
    if strict is not None:
        w = jnp.where(strict, w, 0.0)
    acc = acc + jnp.dot(vt_blk, w.astype(BF16), preferred_element_type=F32)
    carry = carry + jnp.sum(log_keep, axis=0, keepdims=True)
    return carry, acc


def _sb_prompt_kernel(q_ref, k_ref, vt_ref, o_ref):
    i = pl.program_id(2)
    tb = q_ref.shape[1]
    q = q_ref[0]
    strict = _strict_upper(tb)
    upper = jnp.where(strict, 1.0, 0.0).astype(BF16)
    qh = [jnp.where(_head_mask(q.shape, h, 1), q, jnp.zeros_like(q)) for h in range(HEADS_PER_LANE_BLOCK)]

    def visit(j, carries, accs, masked):
        kb = k_ref[0, pl.ds(pl.multiple_of(j * tb, tb), tb), :]
        vtb = vt_ref[0, j]
        outs = [_sb_block_t(lax.dot_general(kb, qh[h], _NT, preferred_element_type=F32),
                            carries[h], accs[h], vtb, strict if masked else None, upper)
                for h in range(HEADS_PER_LANE_BLOCK)]
        return tuple(o[0] for o in outs), tuple(o[1] for o in outs)

    zc = jnp.zeros((1, tb), F32)
    za = jnp.zeros((LANES, tb), F32)
    carries, accs = visit(i, (zc,) * HEADS_PER_LANE_BLOCK, (za,) * HEADS_PER_LANE_BLOCK, True)

    def cond(st):
        jj, carries, _ = st
        live = functools.reduce(jnp.maximum, [jnp.max(c) for c in carries]) > SB_EXIT
        return jnp.logical_and(jj < i, live)

    def body(st):
        jj, carries, accs = st
        carries, accs = visit(i - 1 - jj, carries, accs, False)
        return jj + 1, carries, accs

    _, carries, accs = lax.while_loop(cond, body, (jnp.int32(0), carries, accs))
    rows = lax.broadcasted_iota(jnp.int32, (LANES, tb), 0)
    acc_t = jnp.where(rows < HEAD_DIM, accs[0], accs[1])
    o_ref[0] = acc_t.T.astype(o_ref.dtype)


def _sb_prompt(qb, kb, vt):
    b, t, _ = qb.shape
    tb = ATT_BLOCK
    n_pairs = SB_WIDTH // LANES
    return pl.pallas_call(
        _sb_prompt_kernel,
        grid=(b, n_pairs, t // tb),
        in_specs=[pl.BlockSpec((1, tb, LANES), lambda bi, hp, i: (bi, i, hp)),
                  pl.BlockSpec((1, t, LANES), lambda bi, hp, i: (bi, 0, hp)),
                  pl.BlockSpec((1, t // tb, LANES, tb), lambda bi, hp, i: (bi, 0, hp, 0))],
        out_specs=pl.BlockSpec((1, tb, LANES), lambda bi, hp, i: (bi, i, hp)),
        out_shape=jax.ShapeDtypeStruct((b, t, SB_WIDTH), BF16),
        compiler_params=_cparams("parallel", "parallel", "parallel"), name="sb_prompt",
    )(qb, kb, vt)


def _fold_heads(o_full, n_new):
    row_head = lax.broadcasted_iota(jnp.int32, (o_full.shape[0], HEAD_DIM), 0) // n_new
    out = jnp.zeros((o_full.shape[0], HEAD_DIM), F32)
    for h in range(o_full.shape[1] // HEAD_DIM):
        out = jnp.where(row_head == h, o_full[:, h * HEAD_DIM:(h + 1) * HEAD_DIM], out)
    return out


def _sb_keys(q, kt, vt, valid, carry, acc, lower):
    s = jnp.dot(q, kt, preferred_element_type=F32)
    log_keep = _log_sigmoid_neg(s)
    if valid is not None:
        log_keep = jnp.where(valid, log_keep, 0.0)
    hi, lo = _split_bf16(log_keep)
    after = (jnp.dot(hi, lower, preferred_element_type=F32)
             + jnp.dot(lo, lower, preferred_element_type=F32))
    w = jnp.exp(s + log_keep + after + carry)
    if valid is not None:
        w = jnp.where(valid, w, 0.0)
    acc = acc + lax.dot_general(w.astype(BF16), vt, _NT, preferred_element_type=F32)
    carry = carry + jnp.sum(log_keep, axis=1, keepdims=True)
    return carry, acc


def _later_lane(n):
    r = lax.broadcasted_iota(jnp.int32, (n, n), 0)
    c = lax.broadcasted_iota(jnp.int32, (n, n), 1)
    return jnp.where(r > c, 1.0, 0.0).astype(BF16)


def _sb_sample_kernel(pt_ref, q_ref, kn_ref, vn_ref, kc_hbm, vc_hbm, o_ref, kbuf, vbuf, sem,
                      *, layer, n_new, n_pages):
    b = pl.program_id(0)
    q = q_ref[0]
    rows, width = q.shape
    _, heads, _, page = kbuf.shape

    def copies(p, slot):
        pid = pt_ref[b, p]
        src = lambda ref: ref.at[layer, pid, pl.ds(0, heads), pl.ds(0, HEAD_DIM), pl.ds(0, page)]
        return (pltpu.make_async_copy(src(kc_hbm), kbuf.at[slot], sem.at[0, slot]),
                pltpu.make_async_copy(src(vc_hbm), vbuf.at[slot], sem.at[1, slot]))

    def start(p, slot):
        for c in copies(p, slot):
            c.start()

    def wait(p, slot):
        for c in copies(p, slot):
            c.wait()

    start(n_pages - 1, 0)

    n = kn_ref.shape[2]
    key = lax.broadcasted_iota(jnp.int32, (rows, n), 1)
    tok = lax.broadcasted_iota(jnp.int32, (rows, n), 0) % n_new
    carry, acc = _sb_keys(q, kn_ref[0].astype(BF16), vn_ref[0].astype(BF16), key < tok,
                          jnp.zeros((rows, 1), F32), jnp.zeros((rows, width), F32), _later_lane(n))
    lower = _later_lane(page)

    def cond(st):
        k, carry, _ = st
        return jnp.logical_and(k < n_pages, jnp.max(carry) > SB_EXIT)

    def body(st):
        k, carry, acc = st
        p = n_pages - 1 - k
        slot = k % 2
        wait(p, slot)

        @pl.when(k + 1 < n_pages)
        def _():
            start(p - 1, 1 - slot)

        kt = kbuf[slot].reshape(width, page).astype(BF16)
        vt = vbuf[slot].reshape(width, page).astype(BF16)
        carry, acc = _sb_keys(q, kt, vt, None, carry, acc, lower)
        return k + 1, carry, acc

    k, carry, acc = lax.while_loop(cond, body, (jnp.int32(0), carry, acc))

    @pl.when(k < n_pages)
    def _():
        wait(n_pages - 1 - k, k % 2)

    o_ref[0] = _fold_heads(acc, n_new)


def _sb_sample(layer, page_table, q_rows, kt_new, vt_new, cache_kt, cache_vt, n_new):
    bs, rows, width = q_rows.shape
    n_pages = page_table.shape[1]
    page = cache_kt.shape[4]
    new_lanes = kt_new.shape[2]
    grid_spec = pltpu.PrefetchScalarGridSpec(
        num_scalar_prefetch=1, grid=(bs,),
        in_specs=[pl.BlockSpec((1, rows, width), lambda b, pt: (b, 0, 0)),
                  pl.BlockSpec((1, width, new_lanes), lambda b, pt: (b, 0, 0)),
                  pl.BlockSpec((1, width, new_lanes), lambda b, pt: (b, 0, 0)),
                  pl.BlockSpec(memory_space=pl.ANY), pl.BlockSpec(memory_space=pl.ANY)],
        out_specs=pl.BlockSpec((1, rows, HEAD_DIM), lambda b, pt: (b, 0, 0)),
        scratch_shapes=[pltpu.VMEM((2, SB_HEADS, HEAD_DIM, page), F32),
                        pltpu.VMEM((2, SB_HEADS, HEAD_DIM, page), F32),
                        pltpu.SemaphoreType.DMA((2, 2))])
    return pl.pallas_call(
        functools.partial(_sb_sample_kernel, layer=layer, n_new=n_new, n_pages=n_pages),
        grid_spec=grid_spec, out_shape=jax.ShapeDtypeStruct((bs, rows, HEAD_DIM), F32),
        compiler_params=_cparams("arbitrary"), name="sb_sample",
    )(page_table, q_rows, kt_new, vt_new, cache_kt, cache_vt)


def _top_rows(gate, valid, n_sel, axis):
    idx = lax.broadcasted_iota(jnp.int32, gate.shape, axis)
    size = gate.shape[axis]
    cur = jnp.where(valid, gate, -jnp.inf)
    sel = jnp.zeros(gate.shape, jnp.bool_)
    for _ in range(n_sel):
        m = jnp.max(cur, axis=axis, keepdims=True)
        first = jnp.min(jnp.where(cur == m, idx, size), axis=axis, keepdims=True)
        pick = (idx == first) & valid
        sel = sel | pick
        cur = jnp.where(pick, -jnp.inf, cur)
    return sel


def _moba_prompt_kernel(q_ref, k_ref, vt_ref, km_ref, o_ref, sel_ref):
    i = pl.program_id(2)
    tb = q_ref.shape[1]
    nb = km_ref.shape[1]
    nh = HEADS_PER_LANE_BLOCK
    q = q_ref[0]
    km_hi, km_lo = _split_bf16(km_ref[0])
    key = lax.broadcasted_iota(jnp.int32, (tb, tb), 0)
    qry = lax.broadcasted_iota(jnp.int32, (tb, tb), 1)
    causal = key <= qry
    blk = lax.broadcasted_iota(jnp.int32, (nb, tb), 0)
    qh = [jnp.where(_head_mask(q.shape, h, 1), q, jnp.zeros_like(q)) for h in range(nh)]
    for h in range(nh):
        gate = (lax.dot_general(km_hi, qh[h], _NT, preferred_element_type=F32)
                + lax.dot_general(km_lo, qh[h], _NT, preferred_element_type=F32))
        sel_ref[h, 0:nb, :] = jnp.where(_top_rows(gate, blk < i, MOBA_TOPK, 0), 1.0, 0.0)

    def keys(j):
        return k_ref[0, pl.ds(pl.multiple_of(j * tb, tb), tb), :]

    kb, vtb = keys(i), vt_ref[0, i]
    state = []
    for h in range(nh):
        s_t = jnp.where(causal, lax.dot_general(kb, qh[h], _NT, preferred_element_type=F32), -jnp.inf)
        m = jnp.max(s_t, axis=0, keepdims=True)
        p = jnp.exp(s_t - m)
        state += [m, jnp.sum(p, axis=0, keepdims=True),
                  jnp.dot(vtb, p.astype(BF16), preferred_element_type=F32)]

    def step(j, state):
        kb, vtb = keys(j), vt_ref[0, j]
        out = []
        for h in range(nh):
            m, l, acc = state[3 * h:3 * h + 3]
            picked = sel_ref[h, pl.ds(j, 1), :] > 0.5
            s_t = jnp.where(picked, lax.dot_general(kb, qh[h], _NT, preferred_element_type=F32), -jnp.inf)
            m_new = jnp.maximum(m, jnp.max(s_t, axis=0, keepdims=True))
            alpha = jnp.exp(m - m_new)
            p = jnp.exp(s_t - m_new)
            out += [m_new, alpha * l + jnp.sum(p, axis=0, keepdims=True),
                    alpha * acc + jnp.dot(vtb, p.astype(BF16), preferred_element_type=F32)]
        return tuple(out)

    state = lax.fori_loop(0, i // 2, lambda jj, st: step(2 * jj + 1, step(2 * jj, st)), tuple(state))
    state = lax.fori_loop(0, i % 2, lambda _, st: step(i - 1, st), state)
    rows = lax.broadcasted_iota(jnp.int32, (LANES, tb), 0)
    out_t = jnp.where(rows < HEAD_DIM, state[2] / state[1], state[5] / state[4])
    o_ref[0] = out_t.T.astype(o_ref.dtype)


def _moba_prompt(qb, kb, vt, kmean):
    b, t, _ = qb.shape
    tb = ATT_BLOCK
    nb = t // MOBA_BLOCK
    n_pairs = MOBA_WIDTH // LANES
    off = SB_WIDTH // LANES
    return pl.pallas_call(
        _moba_prompt_kernel,
        grid=(b, n_pairs, t // tb),
        in_specs=[pl.BlockSpec((1, tb, LANES), lambda bi, hp, i: (bi, i, off + hp)),
                  pl.BlockSpec((1, t, LANES), lambda bi, hp, i: (bi, 0, off + hp)),
                  pl.BlockSpec((1, t // tb, LANES, tb), lambda bi, hp, i: (bi, 0, off + hp, 0)),
                  pl.BlockSpec((1, nb, LANES), lambda bi, hp, i: (bi, 0, hp))],
        out_specs=pl.BlockSpec((1, tb, LANES), lambda bi, hp, i: (bi, i, hp)),
        out_shape=jax.ShapeDtypeStruct((b, t, MOBA_WIDTH), BF16),
        scratch_shapes=[pltpu.VMEM((HEADS_PER_LANE_BLOCK, -(-nb // SUBLANES) * SUBLANES, tb), F32)],
        compiler_params=_cparams("parallel", "parallel", "parallel"), name="moba_prompt",
    )(qb, kb, vt, kmean)


def _moba_sample_kernel(pt_ref, q_ref, kn_ref, vn_ref, *refs, n_new, n_seq):
    del pt_ref
    pages = [refs[4 * u:4 * u + 4] for u in range(n_seq)]
    o_ref, m_ref, l_ref, g_ref, acc_ref = refs[4 * n_seq:]
    s = pl.program_id(1)
    nbp = pl.num_programs(1) - 1
    rows, width = q_ref.shape[1:]
    page = pages[0][0].shape[2]
    blk_lane = lax.broadcasted_iota(jnp.int32, m_ref.shape[1:], 1)

    @pl.when(s == 0)
    def _():
        m_ref[...] = jnp.full(m_ref.shape, NEG_BIG, F32)
        l_ref[...] = jnp.zeros_like(l_ref)
        g_ref[...] = jnp.zeros_like(g_ref)

    @pl.when(s < nbp)
    def _():
        for u in range(n_seq):
            ka_ref, kb_ref, va_ref, vb_ref = pages[u]
            q = q_ref[u]
            flat = lambda ref: ref[...].reshape(width, page).astype(BF16)
            kt = jnp.concatenate([flat(ka_ref), flat(kb_ref)], axis=1)
            vt = jnp.concatenate([flat(va_ref), flat(vb_ref)], axis=1)
            sc = jnp.dot(q, kt, preferred_element_type=F32)
            m = jnp.max(sc, axis=1, keepdims=True)
            p = jnp.exp(sc - m)
            acc_ref[u, s] = lax.dot_general(p.astype(BF16), vt, _NT, preferred_element_type=F32)
            m_ref[u] = jnp.where(blk_lane == s, m, m_ref[u])
            l_ref[u] = jnp.where(blk_lane == s, jnp.sum(p, axis=1, keepdims=True), l_ref[u])
            g_ref[u] = jnp.where(blk_lane == s, jnp.sum(sc, axis=1, keepdims=True), g_ref[u])

    @pl.when(s == nbp)
    def _():
        for u in range(n_seq):
            q = q_ref[u]
            sel = _top_rows(g_ref[u], blk_lane < nbp, MOBA_TOPK, 1)
            s_own = jnp.dot(q, kn_ref[u].astype(BF16), preferred_element_type=F32)
            key = lax.broadcasted_iota(jnp.int32, s_own.shape, 1)
            tok = lax.broadcasted_iota(jnp.int32, s_own.shape, 0) % n_new
            s_own = jnp.where(key <= tok, s_own, -jnp.inf)
            m_all, l_all = m_ref[u], l_ref[u]
            m_tot = jnp.maximum(jnp.max(s_own, axis=1, keepdims=True),
                                jnp.max(jnp.where(sel, m_all, -jnp.inf), axis=1, keepdims=True))
            a = jnp.where(sel, jnp.exp(m_all - m_tot), 0.0)
            p_own = jnp.exp(s_own - m_tot)
            denom = jnp.sum(a * l_all, axis=1, keepdims=True) + jnp.sum(p_own, axis=1, keepdims=True)
            num = lax.dot_general(p_own.astype(BF16), vn_ref[u].astype(BF16), _NT, preferred_element_type=F32)
            for j in range(acc_ref.shape[1]):
                num = num + a[:, j:j + 1] * acc_ref[u, j]
            o_ref[u] = _fold_heads(num, n_new) / denom


def _moba_sample(layer, page_table, q_rows, kt_new, vt_new, cache_kt, cache_vt, n_new):
    bs, rows, width = q_rows.shape
    n_pages = page_table.shape[1]
    page = cache_kt.shape[4]
    new_lanes = kt_new.shape[2]
    per_blk = MOBA_BLOCK // page
    assert per_blk == 2 and n_pages % per_blk == 0
    nbp = n_pages // per_blk
    assert nbp <= LANES
    n_seq = next(c for c in (4, 2, 1) if bs % c == 0)
    head_blk = SB_HEADS // MOBA_HEADS

    def page_spec(u, which):
        return pl.BlockSpec(
            (None, None, MOBA_HEADS, HEAD_DIM, page),
            lambda b, s, pt: (layer, pt[b * n_seq + u, per_blk * jnp.minimum(s, nbp - 1) + which], head_blk, 0, 0))

    seq = lambda b, s, pt: (b, 0, 0)
    page_specs, page_args = [], []
    for u in range(n_seq):
        page_specs += [page_spec(u, 0), page_spec(u, 1), page_spec(u, 0), page_spec(u, 1)]
        page_args += [cache_kt, cache_kt, cache_vt, cache_vt]
    grid_spec = pltpu.PrefetchScalarGridSpec(
        num_scalar_prefetch=1, grid=(bs // n_seq, nbp + 1),
        in_specs=[pl.BlockSpec((n_seq, rows, width), seq),
                  pl.BlockSpec((n_seq, width, new_lanes), seq),
                  pl.BlockSpec((n_seq, width, new_lanes), seq)] + page_specs,
        out_specs=pl.BlockSpec((n_seq, rows, HEAD_DIM), seq),
        scratch_shapes=[pltpu.VMEM((n_seq, rows, LANES), F32), pltpu.VMEM((n_seq, rows, LANES), F32),
                        pltpu.VMEM((n_seq, rows, LANES), F32),
                        pltpu.VMEM((n_seq, nbp, rows, width), F32)])
    return pl.pallas_call(
        functools.partial(_moba_sample_kernel, n_new=n_new, n_seq=n_seq),
        grid_spec=grid_spec, out_shape=jax.ShapeDtypeStruct((bs, rows, HEAD_DIM), F32),
        compiler_params=_cparams("parallel", "arbitrary"), name="moba_sample",
    )(page_table, q_rows, kt_new, vt_new, *page_args)


def _layer_norm(t, g, b):
    mu = jnp.mean(t, axis=-1, keepdims=True)
    tc = t - mu
    var = jnp.mean(tc * tc, axis=-1, keepdims=True)
    return tc * lax.rsqrt(var + LN_EPS) * g + b


def _post_attn_kernel(osb_ref, omb_ref, g_ref, x_ref, wsb_ref, wmb_ref, wout_ref, lg_ref, lb_ref,
                      h_ref, hb_ref, hp_ref, *, alpha):
    d = x_ref.shape[1]
    bsb = jnp.dot(osb_ref[...], wsb_ref[...], preferred_element_type=F32)
    bmb = jnp.dot(omb_ref[...], wmb_ref[...], preferred_element_type=F32)
    merged = (jax.nn.sigmoid(g_ref[:, :d].astype(F32)) * bsb
              + jax.nn.sigmoid(g_ref[:, d:].astype(F32)) * bmb)
    t = alpha * x_ref[...] + jnp.dot(merged.astype(BF16), wout_ref[...], preferred_element_type=F32)
    h = _layer_norm(t, lg_ref[...], lb_ref[...])
    h_ref[...] = h
    hb_ref[...] = h.astype(BF16)
    hp_ref[...] = _pack_pairs(h)


def _post_attn(osb, omb, g, x2d, w, alpha):
    n, d = x2d.shape
    tm = min(ROW_TILE, n)
    row = lambda i: (i, 0)
    const = lambda i: (0, 0)
    return pl.pallas_call(
        functools.partial(_post_attn_kernel, alpha=alpha),
        grid=(n // tm,),
        in_specs=[pl.BlockSpec((tm, SB_WIDTH), row), pl.BlockSpec((tm, MOBA_WIDTH), row),
                  pl.BlockSpec((tm, 2 * d), row), pl.BlockSpec((tm, d), row),
                  pl.BlockSpec((SB_WIDTH, d), const), pl.BlockSpec((MOBA_WIDTH, d), const),
                  pl.BlockSpec((d, d), const), pl.BlockSpec((1, d), const), pl.BlockSpec((1, d), const)],
        out_specs=[pl.BlockSpec((tm, d), row), pl.BlockSpec((tm, d), row), pl.BlockSpec((tm, d // 2), row)],
        out_shape=[jax.ShapeDtypeStruct((n, d), F32), jax.ShapeDtypeStruct((n, d), BF16),
                   jax.ShapeDtypeStruct((n, d // 2), U32)],
        compiler_params=_cparams("parallel"), name="post_attn",
    )(osb, omb, g, x2d, w["wsb"], w["wmb"], w["wout"], w["ln1_g"], w["ln1_b"])


def _router_kernel(h_ref, wh_ref, wl_ref, b_ref, e_ref, g_ref):
    hh, hl = _split_bf16(h_ref[...])
    logits = (lax.dot_general(wh_ref[...], hh, _NT, preferred_element_type=F32)
              + lax.dot_general(wh_ref[...], hl, _NT, preferred_element_type=F32)
              + lax.dot_general(wl_ref[...], hh, _NT, preferred_element_type=F32))
    scores = jax.nn.sigmoid(logits)
    biased = scores + b_ref[...]
    n_exp, _ = scores.shape
    per_group = n_exp // N_EXPERT_GROUPS
    group_scores = []
    for g in range(N_EXPERT_GROUPS):
        v = biased[g * per_group:(g + 1) * per_group]
        top2 = _top_rows(v, jnp.ones(v.shape, jnp.bool_), 2, 0)
        group_scores.append(jnp.sum(jnp.where(top2, v, 0.0), axis=0, keepdims=True))
    group_scores = jnp.concatenate(group_scores, axis=0)
    keep = _top_rows(group_scores, jnp.ones(group_scores.shape, jnp.bool_), TOPK_GROUPS, 0)
    masked = jnp.concatenate(
        [jnp.where(keep[g:g + 1, :], biased[g * per_group:(g + 1) * per_group], -jnp.inf)
         for g in range(N_EXPERT_GROUPS)], axis=0)
    row = lax.broadcasted_iota(jnp.int32, masked.shape, 0)
    idxs, gates = [], []
    for _ in range(EXPERT_TOPK):
        m = jnp.max(masked, axis=0, keepdims=True)
        first = jnp.min(jnp.where(masked == m, row, n_exp), axis=0, keepdims=True)
        pick = row == first
        idxs.append(first)
        gates.append(jnp.sum(jnp.where(pick, scores, 0.0), axis=0, keepdims=True))
        masked = jnp.where(pick, -jnp.inf, masked)
    gate = jnp.concatenate(gates, axis=0)
    gate = gate / jnp.sum(gate, axis=0, keepdims=True) * ROUTED_SCALE
    e_ref[...] = jnp.concatenate(idxs, axis=0)
    g_ref[...] = gate


def _router(h, w):
    n, d = h.shape
    tm = min(ROW_TILE, n)
    n_exp = w["wr_hi"].shape[0]
    const = lambda i: (0, 0)
    return pl.pallas_call(
        _router_kernel, grid=(n // tm,),
        in_specs=[pl.BlockSpec((tm, d), lambda i: (i, 0)), pl.BlockSpec((n_exp, d), const),
                  pl.BlockSpec((n_exp, d), const), pl.BlockSpec((n_exp, 1), const)],
        out_specs=[pl.BlockSpec((EXPERT_TOPK, tm), lambda i: (0, i))] * 2,
        out_shape=[jax.ShapeDtypeStruct((EXPERT_TOPK, n), jnp.int32),
                   jax.ShapeDtypeStruct((EXPERT_TOPK, n), F32)],
        compiler_params=_cparams("parallel"), name="router",
    )(h, w["wr_hi"], w["wr_lo"], w["b_router"])


def _experts_kernel(wb_ref, we_ref, ws_ref, wt_ref, x_ref, gate_ref, wgu_ref, wdn_ref, y_ref):
    del we_ref
    w = pl.program_id(0)
    blk = wb_ref[w]
    start, end = ws_ref[w], wt_ref[w]
    rb, half = x_ref.shape

    @pl.when(jnp.logical_or(w == 0, wb_ref[jnp.maximum(w - 1, 0)] != blk))
    def _():
        y_ref[...] = jnp.zeros_like(y_ref)

    @pl.when(end > start)
    def _():
        de = wdn_ref.shape[1]
        x_lo, x_hi = _unpack_pairs(x_ref[...])
        hgu = (jnp.dot(x_lo.astype(BF16), wgu_ref[0, :half, :], preferred_element_type=F32)
               + jnp.dot(x_hi.astype(BF16), wgu_ref[0, half:, :], preferred_element_type=F32))
        act = jax.nn.silu(hgu[:, :de]) * hgu[:, de:]
        y = jnp.dot(act.astype(BF16), wdn_ref[0], preferred_element_type=F32) * gate_ref[...]
        r = blk * rb + lax.broadcasted_iota(jnp.int32, (rb, 1), 0)
        mine = jnp.logical_and(r >= start, r < end)
        y_ref[...] = jnp.where(mine, _pack_pairs(y), y_ref[...])


def _experts(work, xs, row_gate, w_gu, w_dn):
    rows, half = xs.shape
    rb = MOE_ROW_BLOCK
    d, de2 = w_gu.shape[1:]
    n_work = work[0].shape[0]
    grid_spec = pltpu.PrefetchScalarGridSpec(
        num_scalar_prefetch=4, grid=(n_work,),
        in_specs=[pl.BlockSpec((rb, half), lambda w, wb, we, ws, wt: (wb[w], 0)),
                  pl.BlockSpec((rb, 1), lambda w, wb, we, ws, wt: (wb[w], 0)),
                  pl.BlockSpec((1, d, de2), lambda w, wb, we, ws, wt: (we[w], 0, 0)),
                  pl.BlockSpec((1, de2 // 2, d), lambda w, wb, we, ws, wt: (we[w], 0, 0))],
        out_specs=pl.BlockSpec((rb, half), lambda w, wb, we, ws, wt: (wb[w], 0)))
    return pl.pallas_call(
        _experts_kernel, grid_spec=grid_spec, out_shape=jax.ShapeDtypeStruct((rows, half), U32),
        compiler_params=_cparams("arbitrary"), name="experts",
    )(*work, xs, row_gate, w_gu, w_dn)


def _final_kernel(h_ref, hb_ref, r_ref, p_ref, wsg_ref, wsd_ref, wpg_ref, wpp_ref, lg_ref, lb_ref,
                  y_ref, *, alpha):
    hb = hb_ref[...]
    de = wsd_ref.shape[0]
    sgu = jnp.dot(hb, wsg_ref[...], preferred_element_type=F32)
    shared = jnp.dot((jax.nn.silu(sgu[:, :de]) * sgu[:, de:]).astype(BF16), wsd_ref[...],
                     preferred_element_type=F32)
    ple = (jax.nn.sigmoid(jnp.dot(hb, wpg_ref[...], preferred_element_type=F32))
           * jnp.dot(p_ref[...].astype(BF16), wpp_ref[...], preferred_element_type=F32))
    parts = [_unpack_pairs(r_ref[k]) for k in range(r_ref.shape[0])]
    routed = jnp.concatenate([functools.reduce(jnp.add, [p[0] for p in parts]),
                              functools.reduce(jnp.add, [p[1] for p in parts])], axis=1)
    t = alpha * h_ref[...] + (routed + shared) + ple
    y_ref[...] = _layer_norm(t, lg_ref[...], lb_ref[...])


def _final(h, hb, routed_rows, routed_row0, p2d, w, alpha):
    n, d = h.shape
    tm = min(ROW_TILE, n)
    assert routed_row0 % tm == 0
    r_off = routed_row0 // tm
    k = routed_rows.shape[0]
    pd = p2d.shape[1]
    de2 = w["wsg"].shape[1]
    row = lambda i: (i, 0)
    const = lambda i: (0, 0)
    return pl.pallas_call(
        functools.partial(_final_kernel, alpha=alpha),
        grid=(n // tm,),
        in_specs=[pl.BlockSpec((tm, d), row), pl.BlockSpec((tm, d), row),
                  pl.BlockSpec((k, tm, d // 2), lambda i: (0, i + r_off, 0)), pl.BlockSpec((tm, pd), row),
                  pl.BlockSpec((d, de2), const), pl.BlockSpec((de2 // 2, d), const),
                  pl.BlockSpec((d, d), const), pl.BlockSpec((pd, d), const),
                  pl.BlockSpec((1, d), const), pl.BlockSpec((1, d), const)],
        out_specs=pl.BlockSpec((tm, d), row),
        out_shape=jax.ShapeDtypeStruct((n, d), F32),
        compiler_params=_cparams("parallel"), name="final",
    )(h, hb, routed_rows, p2d, w["wsg"], w["wsd"], w["wpg"], w["wpp"], w["ln2_g"], w["ln2_b"])


def _gather_rows(table, idx):
    return table.at[idx].get(mode="promise_in_bounds")


def _moe_routed(hp, top_e_t, gate_t, w_gu, w_dn):
    n = hp.shape[0]
    n_exp = w_gu.shape[0]
    k = top_e_t.shape[0]
    rb = MOE_ROW_BLOCK
    n_assign = n * k
    n_blocks = -(-n_assign // rb)
    rows = n_blocks * rb
    ids = jnp.arange(n_assign, dtype=jnp.int32)
    flat_e = top_e_t.reshape(-1)
    _, sorted_id, sorted_gate = lax.sort((flat_e, ids, gate_t.reshape(-1)), num_keys=1, is_stable=True)
    _, pos = lax.sort((sorted_id, ids), num_keys=1)
    sorted_tok = jnp.pad(sorted_id % n, (0, rows - n_assign))
    sorted_gate = jnp.pad(sorted_gate, (0, rows - n_assign))

    counts = jnp.sum((flat_e[:, None] == jnp.arange(n_exp, dtype=jnp.int32)[None, :]).astype(jnp.int32), axis=0)
    group_end = jnp.cumsum(counts)
    starts = jnp.sort(jnp.concatenate([jnp.arange(n_blocks, dtype=jnp.int32) * rb,
                                       group_end[:n_exp - 1].astype(jnp.int32)]))
    ends = jnp.concatenate([starts[1:], jnp.full((1,), n_assign, jnp.int32)])
    work_blk = jnp.minimum(starts // rb, n_blocks - 1)
    work_exp = jnp.minimum(jnp.sum((group_end[None, :] <= starts[:, None]).astype(jnp.int32), axis=1), n_exp - 1)

    xs = _gather_rows(hp, sorted_tok)
    y = _experts((work_blk, work_exp, starts, ends), xs, sorted_gate[:, None], w_gu, w_dn)
    return _gather_rows(y, pos.reshape(k, n))


def _expand_heads(q2d, n_seq, n_new, n_heads):
    q = q2d.reshape(n_seq, n_new, n_heads, HEAD_DIM)
    eye = jnp.eye(n_heads, dtype=q2d.dtype)
    return jnp.einsum("bthd,hg->bhtgd", q, eye).reshape(n_seq, n_heads * n_new, n_heads * HEAD_DIM)


def _token_major(o, n_seq, n_new, n_heads):
    o = o.reshape(n_seq, n_heads, n_new, HEAD_DIM)
    return jnp.transpose(o, (0, 2, 1, 3)).reshape(n_seq * n_new, n_heads * HEAD_DIM)


def _new_keys_t(kv2d, n_seq, n_new, n_lanes):
    a = jnp.transpose(kv2d.reshape(n_seq, n_new, -1), (0, 2, 1))
    return jnp.pad(a, ((0, 0), (0, 0), (0, n_lanes - n_new)))


def _prep_weights(i, w_in, w_branch_sb, w_branch_moba, w_out, ln1_g, ln1_b, w_router, b_router,
                  w_sh_gu, w_sh_down, w_ple_gate, w_ple_proj, ln2_g, ln2_b):
    wi = w_in[i]
    d = wi.shape[0]
    s, m = SB_WIDTH, MOBA_WIDTH
    q_sb, k_sb, v_sb = wi[:, 0:s], wi[:, s:2 * s], wi[:, 2 * s:3 * s]
    o = 3 * s
    q_mb, k_mb, v_mb = wi[:, o:o + m], wi[:, o + m:o + 2 * m], wi[:, o + 2 * m:o + 3 * m]
    wv = jnp.concatenate([v_sb, v_mb], axis=1).astype(BF16)
    wr_hi, wr_lo = _split_bf16(w_router[i].T)
    return {
        "wq": jnp.concatenate([q_sb, q_mb], axis=1).astype(BF16),
        "wk": jnp.concatenate([k_sb, k_mb], axis=1).astype(BF16),
        "wv": wv, "wvt": wv.T, "wg": wi[:, o + 3 * m:].astype(BF16),
        "wsb": w_branch_sb[i].astype(BF16), "wmb": w_branch_moba[i].astype(BF16),
        "wout": w_out[i].astype(BF16),
        "ln1_g": ln1_g[i].reshape(1, d), "ln1_b": ln1_b[i].reshape(1, d),
        "wr_hi": wr_hi, "wr_lo": wr_lo, "b_router": b_router[i].reshape(-1, 1).astype(F32),
        "wsg": w_sh_gu[i].astype(BF16), "wsd": w_sh_down[i].astype(BF16),
        "wpg": w_ple_gate[i].astype(BF16), "wpp": w_ple_proj[i].astype(BF16),
        "ln2_g": ln2_g[i].reshape(1, d), "ln2_b": ln2_b[i].reshape(1, d),
    }


def kernel(x_prompt, x_sample, cache_k, cache_v, page_table, p_prompt, p_sample, w_in, w_branch_sb, w_branch_moba, w_out, ln1_g, ln1_b, w_router, b_router, w_exp_gu, w_exp_down, w_sh_gu, w_sh_down, w_ple_gate, w_ple_proj, ln2_g, ln2_b):
    depth = w_in.shape[0]
    alpha = (2 * depth) ** 0.25
    bp, tp, d = x_prompt.shape
    bs, ts, _ = x_sample.shape
    n_p, n_s = bp * tp, bs * ts
    n_pages = page_table.shape[1]
    page = cache_k.shape[2]
    past = n_pages * page
    width = SB_WIDTH + MOBA_WIDTH
    n_heads = SB_HEADS + MOBA_HEADS
    assert SB_HEADS == MOBA_HEADS and cache_k.shape[3] == n_heads
    assert tp % ATT_BLOCK == 0 and past % MOBA_BLOCK == 0 and ts <= LANES
    assert n_p % min(ROW_TILE, n_s) == 0

    tab_p = _rope_tables(jnp.arange(tp))
    tab_s = _rope_tables(past + (jnp.arange(n_s) % ts))
    cache_kt = jnp.transpose(cache_k, (0, 1, 3, 4, 2))
    cache_vt = jnp.transpose(cache_v, (0, 1, 3, 4, 2))

    h_p = x_prompt.reshape(n_p, d)
    h_s = x_sample.reshape(n_s, d)
    k_p_rows, v_p_rows, k_s_rows, v_s_rows = [], [], [], []
    for i in range(depth):
        w = _prep_weights(i, w_in, w_branch_sb, w_branch_moba, w_out, ln1_g, ln1_b, w_router, b_router,
                          w_sh_gu, w_sh_down, w_ple_gate, w_ple_proj, ln2_g, ln2_b)

        kf, vf, qb, kb, g_p, vt, km = _inproj(h_p, w, tab_p, tp, True)
        k_p_rows.append(kf.reshape(bp, tp, n_heads, HEAD_DIM))
        v_p_rows.append(vf.reshape(bp, tp, n_heads, HEAD_DIM))
        qb3, kb3 = qb.reshape(bp, tp, width), kb.reshape(bp, tp, width)
        osb_p = _sb_prompt(qb3, kb3, vt).reshape(n_p, SB_WIDTH)
        omb_p = _moba_prompt(qb3, kb3, vt, km.reshape(bp, tp // MOBA_BLOCK, MOBA_WIDTH)).reshape(n_p, MOBA_WIDTH)
        h1_p, h1b_p, h1p_p = _post_attn(osb_p, omb_p, g_p, h_p, w, alpha)

        kf_s, vf_s, qb_s, g_s = _inproj(h_s, w, tab_s, ts, False)
        k_s_rows.append(kf_s.reshape(bs, ts, n_heads, HEAD_DIM))
        v_s_rows.append(vf_s.reshape(bs, ts, n_heads, HEAD_DIM))
        o_sb = _sb_sample(i, page_table, _expand_heads(qb_s[:, :SB_WIDTH], bs, ts, SB_HEADS),
                          _new_keys_t(kf_s[:, :SB_WIDTH], bs, ts, LANES),
                          _new_keys_t(vf_s[:, :SB_WIDTH], bs, ts, LANES), cache_kt, cache_vt, ts)
        o_mb = _moba_sample(i, page_table, _expand_heads(qb_s[:, SB_WIDTH:], bs, ts, MOBA_HEADS),
                            _new_keys_t(kf_s[:, SB_WIDTH:], bs, ts, LANES),
                            _new_keys_t(vf_s[:, SB_WIDTH:], bs, ts, LANES), cache_kt, cache_vt, ts)
        osb_s = _token_major(o_sb, bs, ts, SB_HEADS).astype(BF16)
        omb_s = _token_major(o_mb, bs, ts, MOBA_HEADS).astype(BF16)
        h1_s, h1b_s, h1p_s = _post_attn(osb_s, omb_s, g_s, h_s, w, alpha)

        e_p, g_rp = _router(h1_p, w)
        e_s, g_rs = _router(h1_s, w)
        routed_rows = _moe_routed(jnp.concatenate([h1p_p, h1p_s], axis=0),
                                  jnp.concatenate([e_p, e_s], axis=1), jnp.concatenate([g_rp, g_rs], axis=1),
                                  w_exp_gu[i].astype(BF16), w_exp_down[i].astype(BF16))

        h_p = _final(h1_p, h1b_p, routed_rows, 0, p_prompt[i].reshape(n_p, -1), w, alpha)
        h_s = _final(h1_s, h1b_s, routed_rows, n_p, p_sample[i].reshape(n_s, -1), w, alpha)

    return (h_p.reshape(bp, tp, d), h_s.reshape(bs, ts, d),
            jnp.stack(k_p_rows), jnp.stack(v_p_rows), jnp.stack(k_s_rows), jnp.stack(v_s_rows))
```

```python
import functools
import math

import jax
import jax.numpy as jnp
from jax import lax
from jax.experimental import pallas as pl
from jax.experimental.pallas import tpu as pltpu
from jax.experimental.pallas import tpu_sc as plsc

F32 = jnp.float32
BF16 = jnp.bfloat16
U32 = jnp.uint32

HEAD_DIM = 64
SB_HEADS = 8
MOBA_HEADS = 8
SB_WIDTH = SB_HEADS * HEAD_DIM
MOBA_WIDTH = MOBA_HEADS * HEAD_DIM
ROPE_THETA = 10000.0
MOBA_BLOCK = 256
MOBA_TOPK = 3
EXPERT_TOPK = 8
N_EXPERT_GROUPS = 8
TOPK_GROUPS = 4
ROUTED_SCALE = 2.5
LN_EPS = 1e-5

LANES = 128
SUBLANES = 8
HEADS_PER_LANE_BLOCK = LANES // HEAD_DIM
ROW_TILE = 512
ATT_BLOCK = 256
MOE_ROW_BLOCK = 256
SC_GATHER_ROWS = 128
VMEM_LIMIT = 56 * 1024 * 1024
NEG_BIG = -1e30
SB_EXIT = -104.0

_NT = (((1,), (1,)), ((), ()))


def _cparams(*sem):
    return pltpu.CompilerParams(dimension_semantics=sem, vmem_limit_bytes=VMEM_LIMIT)


def _split_bf16(x):
    hi = x.astype(BF16)
    lo = (x - hi.astype(F32)).astype(BF16)
    return hi, lo


def _head_mask(shape, h, lane_axis):
    lane = lax.broadcasted_iota(jnp.int32, shape, lane_axis)
    return (lane >= h * HEAD_DIM) & (lane < (h + 1) * HEAD_DIM)


def _log_sigmoid_neg(z):
    return -(jnp.maximum(z, 0.0) + jnp.log(1.0 + jnp.exp(-jnp.abs(z))))


def _pack_pairs(x):
    k = x.shape[1] // 2
    lo = lax.bitcast_convert_type(x[:, :k].astype(BF16).astype(F32), U32)
    hi = lax.bitcast_convert_type(x[:, k:].astype(BF16).astype(F32), U32)
    return hi | (lo >> 16)


def _unpack_pairs(u):
    lo = lax.bitcast_convert_type(u << 16, F32)
    hi = lax.bitcast_convert_type(u & jnp.uint32(0xFFFF0000), F32)
    return lo, hi


def _rope_lanes(y, cos, s1, s2):
    outs = []
    for c in range(y.shape[1] // LANES):
        yc = y[:, c * LANES:(c + 1) * LANES]
        outs.append(yc * cos + pltpu.roll(yc, LANES - HEAD_DIM // 2, 1) * s1
                    + pltpu.roll(yc, HEAD_DIM // 2, 1) * s2)
    return jnp.concatenate(outs, axis=1)


def _inproj_kernel(x_ref, wq_ref, wk_ref, wv_ref, wg_ref, cos_ref, s1_ref, s2_ref, *refs, prompt):
    if prompt:
        wvt_ref, kf_ref, vf_ref, qb_ref, kb_ref, g_ref, vt_ref, km_ref = refs
    else:
        kf_ref, vf_ref, qb_ref, g_ref = refs
    xb = x_ref[...].astype(BF16)
    cos, s1, s2 = cos_ref[...], s1_ref[...], s2_ref[...]
    scale = 1.0 / math.sqrt(HEAD_DIM)

    yq = jnp.dot(xb, wq_ref[...], preferred_element_type=F32)
    qb_ref[:, :SB_WIDTH] = (yq[:, :SB_WIDTH] * scale).astype(BF16)
    qb_ref[:, SB_WIDTH:] = (_rope_lanes(yq[:, SB_WIDTH:], cos, s1, s2) * scale).astype(BF16)

    yk = jnp.dot(xb, wk_ref[...], preferred_element_type=F32)
    k_mb = _rope_lanes(yk[:, SB_WIDTH:], cos, s1, s2)
    kf_ref[:, :SB_WIDTH] = yk[:, :SB_WIDTH]
    kf_ref[:, SB_WIDTH:] = k_mb

    vf_ref[...] = jnp.dot(xb, wv_ref[...], preferred_element_type=F32)
    g_ref[...] = jnp.dot(xb, wg_ref[...], preferred_element_type=F32).astype(BF16)

    if prompt:
        kb_ref[:, :SB_WIDTH] = yk[:, :SB_WIDTH].astype(BF16)
        kb_ref[:, SB_WIDTH:] = k_mb.astype(BF16)
        rows = xb.shape[0]
        for r in range(rows // MOBA_BLOCK):
            blk = k_mb[r * MOBA_BLOCK:(r + 1) * MOBA_BLOCK]
            km_ref[0, r:r + 1, :] = jnp.sum(blk, axis=0, keepdims=True) * (1.0 / MOBA_BLOCK)
        vt = lax.dot_general(wvt_ref[...], xb, _NT, preferred_element_type=F32)
        for r in range(rows // ATT_BLOCK):
            vt_ref[0, r] = vt[:, r * ATT_BLOCK:(r + 1) * ATT_BLOCK].astype(BF16)


def _rope_tables(positions):
    half = HEAD_DIM // 2
    inv_freq = jnp.power(ROPE_THETA, -jnp.arange(half, dtype=F32) / half)
    ang = positions.astype(F32)[:, None] * inv_freq[None, :]
    cos, sin = jnp.cos(ang), jnp.sin(ang)
    zero = jnp.zeros_like(sin)
    reps = LANES // HEAD_DIM
    cos_t = jnp.tile(jnp.concatenate([cos, cos], axis=1), (1, reps))
    s1_t = jnp.tile(jnp.concatenate([-sin, zero], axis=1), (1, reps))
    s2_t = jnp.tile(jnp.concatenate([zero, sin], axis=1), (1, reps))
    return cos_t, s1_t, s2_t


def _inproj(x2d, w, tables, seq_len, prompt):
    n, d = x2d.shape
    tm = min(ROW_TILE, n)
    assert n % tm == 0
    width = SB_WIDTH + MOBA_WIDTH
    n_tab = tables[0].shape[0] // tm
    row = lambda i: (i, 0)
    const = lambda i: (0, 0)
    tab = lambda i: (i % n_tab, 0)
    in_specs = [pl.BlockSpec((tm, d), row),
                pl.BlockSpec((d, width), const), pl.BlockSpec((d, width), const),
                pl.BlockSpec((d, width), const), pl.BlockSpec((d, 2 * d), const),
                pl.BlockSpec((tm, LANES), tab), pl.BlockSpec((tm, LANES), tab),
                pl.BlockSpec((tm, LANES), tab)]
    args = [x2d, w["wq"], w["wk"], w["wv"], w["wg"], *tables]
    out_shape = [jax.ShapeDtypeStruct((n, width), F32), jax.ShapeDtypeStruct((n, width), F32),
                 jax.ShapeDtypeStruct((n, width), BF16)]
    out_specs = [pl.BlockSpec((tm, width), row)] * 3
    if prompt:
        out_shape.append(jax.ShapeDtypeStruct((n, width), BF16))
        out_specs.append(pl.BlockSpec((tm, width), row))
    out_shape.append(jax.ShapeDtypeStruct((n, 2 * d), BF16))
    out_specs.append(pl.BlockSpec((tm, 2 * d), row))
    if prompt:
        assert tm % ATT_BLOCK == 0 and seq_len % tm == 0
        per_seq = seq_len // tm
        in_specs.append(pl.BlockSpec((width, d), const))
        args.append(w["wvt"])
        out_shape += [jax.ShapeDtypeStruct((n // seq_len, seq_len // ATT_BLOCK, width, ATT_BLOCK), BF16),
                      jax.ShapeDtypeStruct((n // tm, tm // MOBA_BLOCK, MOBA_WIDTH), F32)]
        out_specs += [pl.BlockSpec((1, tm // ATT_BLOCK, width, ATT_BLOCK),
                                   lambda i: (i // per_seq, i % per_seq, 0, 0)),
                      pl.BlockSpec((1, tm // MOBA_BLOCK, MOBA_WIDTH), lambda i: (i, 0, 0))]
    return pl.pallas_call(
        functools.partial(_inproj_kernel, prompt=prompt),
        grid=(n // tm,), in_specs=in_specs, out_specs=out_specs, out_shape=out_shape,
        compiler_params=_cparams("parallel"), name="inproj_prompt" if prompt else "inproj_sample",
    )(*args)


def _strict_upper(n):
    r = lax.broadcasted_iota(jnp.int32, (n, n), 0)
    c = lax.broadcasted_iota(jnp.int32, (n, n), 1)
    return r < c


def _sb_block_t(z_t, carry, acc, vt_blk, strict, upper):
    log_keep = _log_sigmoid_neg(z_t)
    if strict is not None:
        log_keep = jnp.where(strict, log_keep, 0.0)
    hi, lo = _split_bf16(log_keep)
    after = (jnp.dot(upper, hi, preferred_element_type=F32)
             + jnp.dot(upper, lo, preferred_element_type=F32))
    w = jnp.exp(z_t + log_keep + after + carry)
    if strict is not None:
        w = jnp.where(strict, w, 0.0)
    acc = acc + jnp.dot(vt_blk, w.astype(BF16), preferred_element_type=F32)
    carry = carry + jnp.sum(log_keep, axis=0, keepdims=True)
    return carry, acc


def _sb_prompt_kernel(q_ref, k_ref, vt_ref, o_ref):
    i = pl.program_id(2)
    tb = q_ref.shape[1]
    q = q_ref[0]
    strict = _strict_upper(tb)
    upper = jnp.where(strict, 1.0, 0.0).astype(BF16)
    qh = [jnp.where(_head_mask(q.shape, h, 1), q, jnp.zeros_like(q)) for h in range(HEADS_PER_LANE_BLOCK)]

    def visit(j, carries, accs, masked):
        kb = k_ref[0, pl.ds(pl.multiple_of(j * tb, tb), tb), :]
        vtb = vt_ref[0, j]
        outs = [_sb_block_t(lax.dot_general(kb, qh[h], _NT, preferred_element_type=F32),
                            carries[h], accs[h], vtb, strict if masked else None, upper)
                for h in range(HEADS_PER_LANE_BLOCK)]
        return tuple(o[0] for o in outs), tuple(o[1] for o in outs)

    zc = jnp.zeros((1, tb), F32)
    za = jnp.zeros((LANES, tb), F32)
    carries, accs = visit(i, (zc,) * HEADS_PER_LANE_BLOCK, (za,) * HEADS_PER_LANE_BLOCK, True)

    def cond(st):
        jj, carries, _ = st
        live = functools.reduce(jnp.maximum, [jnp.max(c) for c in carries]) > SB_EXIT
        return jnp.logical_and(jj < i, live)

    def body(st):
        jj, carries, accs = st
        carries, accs = visit(i - 1 - jj, carries, accs, False)
        return jj + 1, carries, accs

    _, carries, accs = lax.while_loop(cond, body, (jnp.int32(0), carries, accs))
    rows = lax.broadcasted_iota(jnp.int32, (LANES, tb), 0)
    acc_t = jnp.where(rows < HEAD_DIM, accs[0], accs[1])
    o_ref[0] = acc_t.T.astype(o_ref.dtype)


def _sb_prompt(qb, kb, vt):
    b, t, _ = qb.shape
    tb = ATT_BLOCK
    n_pairs = SB_WIDTH // LANES
    return pl.pallas_call(
        _sb_prompt_kernel,
        grid=(b, n_pairs, t // tb),
        in_specs=[pl.BlockSpec((1, tb, LANES), lambda bi, hp, i: (bi, i, hp)),
                  pl.BlockSpec((1, t, LANES), lambda bi, hp, i: (bi, 0, hp)),
                  pl.BlockSpec((1, t // tb, LANES, tb), lambda bi, hp, i: (bi, 0, hp, 0))],
        out_specs=pl.BlockSpec((1, tb, LANES), lambda bi, hp, i: (bi, i, hp)),
        out_shape=jax.ShapeDtypeStruct((b, t, SB_WIDTH), BF16),
        compiler_params=_cparams("parallel", "parallel", "parallel"), name="sb_prompt",
    )(qb, kb, vt)


def _fold_heads(o_full, n_new):
    row_head = lax.broadcasted_iota(jnp.int32, (o_full.shape[0], HEAD_DIM), 0) // n_new
    out = jnp.zeros((o_full.shape[0], HEAD_DIM), F32)
    for h in range(o_full.shape[1] // HEAD_DIM):
        out = jnp.where(row_head == h, o_full[:, h * HEAD_DIM:(h + 1) * HEAD_DIM], out)
    return out


def _sb_keys(q, kt, vt, valid, carry, acc, lower):
    s = jnp.dot(q, kt, preferred_element_type=F32)
    log_keep = _log_sigmoid_neg(s)
    if valid is not None:
        log_keep = jnp.where(valid, log_keep, 0.0)
    hi, lo = _split_bf16(log_keep)
    after = (jnp.dot(hi, lower, preferred_element_type=F32)
             + jnp.dot(lo, lower, preferred_element_type=F32))
    w = jnp.exp(s + log_keep + after + carry)
    if valid is not None:
        w = jnp.where(valid, w, 0.0)
    acc = acc + lax.dot_general(w.astype(BF16), vt, _NT, preferred_element_type=F32)
    carry = carry + jnp.sum(log_keep, axis=1, keepdims=True)
    return carry, acc


def _later_lane(n):
    r = lax.broadcasted_iota(jnp.int32, (n, n), 0)
    c = lax.broadcasted_iota(jnp.int32, (n, n), 1)
    return jnp.where(r > c, 1.0, 0.0).astype(BF16)


def _sb_sample_kernel(pt_ref, q_ref, kn_ref, vn_ref, kc_hbm, vc_hbm, o_ref, kbuf, vbuf, sem,
                      *, layer, n_new, n_pages):
    b = pl.program_id(0)
    q = q_ref[0]
    rows, width = q.shape
    _, heads, _, page = kbuf.shape

    def copies(p, slot):
        pid = pt_ref[b, p]
        src = lambda ref: ref.at[layer, pid, pl.ds(0, heads), pl.ds(0, HEAD_DIM), pl.ds(0, page)]
        return (pltpu.make_async_copy(src(kc_hbm), kbuf.at[slot], sem.at[0, slot]),
                pltpu.make_async_copy(src(vc_hbm), vbuf.at[slot], sem.at[1, slot]))

    def start(p, slot):
        for c in copies(p, slot):
            c.start()

    def wait(p, slot):
        for c in copies(p, slot):
            c.wait()

    start(n_pages - 1, 0)

    n = kn_ref.shape[2]
    key = lax.broadcasted_iota(jnp.int32, (rows, n), 1)
    tok = lax.broadcasted_iota(jnp.int32, (rows, n), 0) % n_new
    carry, acc = _sb_keys(q, kn_ref[0].astype(BF16), vn_ref[0].astype(BF16), key < tok,
                          jnp.zeros((rows, 1), F32), jnp.zeros((rows, width), F32), _later_lane(n))
    lower = _later_lane(page)

    def cond(st):
        k, carry, _ = st
        return jnp.logical_and(k < n_pages, jnp.max(carry) > SB_EXIT)

    def body(st):
        k, carry, acc = st
        p = n_pages - 1 - k
        slot = k % 2
        wait(p, slot)

        @pl.when(k + 1 < n_pages)
        def _():
            start(p - 1, 1 - slot)

        kt = kbuf[slot].reshape(width, page).astype(BF16)
        vt = vbuf[slot].reshape(width, page).astype(BF16)
        carry, acc = _sb_keys(q, kt, vt, None, carry, acc, lower)
        return k + 1, carry, acc

    k, carry, acc = lax.while_loop(cond, body, (jnp.int32(0), carry, acc))

    @pl.when(k < n_pages)
    def _():
        wait(n_pages - 1 - k, k % 2)

    o_ref[0] = _fold_heads(acc, n_new)


def _sb_sample(layer, page_table, q_rows, kt_new, vt_new, cache_kt, cache_vt, n_new):
    bs, rows, width = q_rows.shape
    n_pages = page_table.shape[1]
    page = cache_kt.shape[4]
    new_lanes = kt_new.shape[2]
    grid_spec = pltpu.PrefetchScalarGridSpec(
        num_scalar_prefetch=1, grid=(bs,),
        in_specs=[pl.BlockSpec((1, rows, width), lambda b, pt: (b, 0, 0)),
                  pl.BlockSpec((1, width, new_lanes), lambda b, pt: (b, 0, 0)),
                  pl.BlockSpec((1, width, new_lanes), lambda b, pt: (b, 0, 0)),
                  pl.BlockSpec(memory_space=pl.ANY), pl.BlockSpec(memory_space=pl.ANY)],
        out_specs=pl.BlockSpec((1, rows, HEAD_DIM), lambda b, pt: (b, 0, 0)),
        scratch_shapes=[pltpu.VMEM((2, SB_HEADS, HEAD_DIM, page), F32),
                        pltpu.VMEM((2, SB_HEADS, HEAD_DIM, page), F32),
                        pltpu.SemaphoreType.DMA((2, 2))])
    return pl.pallas_call(
        functools.partial(_sb_sample_kernel, layer=layer, n_new=n_new, n_pages=n_pages),
        grid_spec=grid_spec, out_shape=jax.ShapeDtypeStruct((bs, rows, HEAD_DIM), F32),
        compiler_params=_cparams("arbitrary"), name="sb_sample",
    )(page_table, q_rows, kt_new, vt_new, cache_kt, cache_vt)


def _top_rows(gate, valid, n_sel, axis):
    idx = lax.broadcasted_iota(jnp.int32, gate.shape, axis)
    size = gate.shape[axis]
    cur = jnp.where(valid, gate, -jnp.inf)
    sel = jnp.zeros(gate.shape, jnp.bool_)
    for _ in range(n_sel):
        m = jnp.max(cur, axis=axis, keepdims=True)
        first = jnp.min(jnp.where(cur == m, idx, size), axis=axis, keepdims=True)
        pick = (idx == first) & valid
        sel = sel | pick
        cur = jnp.where(pick, -jnp.inf, cur)
    return sel


def _moba_prompt_kernel(q_ref, k_ref, vt_ref, km_ref, o_ref, sel_ref):
    i = pl.program_id(2)
    tb = q_ref.shape[1]
    nb = km_ref.shape[1]
    nh = HEADS_PER_LANE_BLOCK
    q = q_ref[0]
    km_hi, km_lo = _split_bf16(km_ref[0])
    key = lax.broadcasted_iota(jnp.int32, (tb, tb), 0)
    qry = lax.broadcasted_iota(jnp.int32, (tb, tb), 1)
    causal = key <= qry
    blk = lax.broadcasted_iota(jnp.int32, (nb, tb), 0)
    qh = [jnp.where(_head_mask(q.shape, h, 1), q, jnp.zeros_like(q)) for h in range(nh)]
    for h in range(nh):
        gate = (lax.dot_general(km_hi, qh[h], _NT, preferred_element_type=F32)
                + lax.dot_general(km_lo, qh[h], _NT, preferred_element_type=F32))
        sel_ref[h, 0:nb, :] = jnp.where(_top_rows(gate, blk < i, MOBA_TOPK, 0), 1.0, 0.0)

    def keys(j):
        return k_ref[0, pl.ds(pl.multiple_of(j * tb, tb), tb), :]

    kb, vtb = keys(i), vt_ref[0, i]
    state = []
    for h in range(nh):
        s_t = jnp.where(causal, lax.dot_general(kb, qh[h], _NT, preferred_element_type=F32), -jnp.inf)
        m = jnp.max(s_t, axis=0, keepdims=True)
        p = jnp.exp(s_t - m)
        state += [m, jnp.sum(p, axis=0, keepdims=True),
                  jnp.dot(vtb, p.astype(BF16), preferred_element_type=F32)]

    def step(j, state):
        kb, vtb = keys(j), vt_ref[0, j]
        out = []
        for h in range(nh):
            m, l, acc = state[3 * h:3 * h + 3]
            picked = sel_ref[h, pl.ds(j, 1), :] > 0.5
            s_t = jnp.where(picked, lax.dot_general(kb, qh[h], _NT, preferred_element_type=F32), -jnp.inf)
            m_new = jnp.maximum(m, jnp.max(s_t, axis=0, keepdims=True))
            alpha = jnp.exp(m - m_new)
            p = jnp.exp(s_t - m_new)
            out += [m_new, alpha * l + jnp.sum(p, axis=0, keepdims=True),
                    alpha * acc + jnp.dot(vtb, p.astype(BF16), preferred_element_type=F32)]
        return tuple(out)

    state = lax.fori_loop(0, i // 2, lambda jj, st: step(2 * jj + 1, step(2 * jj, st)), tuple(state))
    state = lax.fori_loop(0, i % 2, lambda _, st: step(i - 1, st), state)
    rows = lax.broadcasted_iota(jnp.int32, (LANES, tb), 0)
    out_t = jnp.where(rows < HEAD_DIM, state[2] / state[1], state[5] / state[4])
    o_ref[0] = out_t.T.astype(o_ref.dtype)


def _moba_prompt(qb, kb, vt, kmean):
    b, t, _ = qb.shape
    tb = ATT_BLOCK
    nb = t // MOBA_BLOCK
    n_pairs = MOBA_WIDTH // LANES
    off = SB_WIDTH // LANES
    return pl.pallas_call(
        _moba_prompt_kernel,
        grid=(b, n_pairs, t // tb),
        in_specs=[pl.BlockSpec((1, tb, LANES), lambda bi, hp, i: (bi, i, off + hp)),
                  pl.BlockSpec((1, t, LANES), lambda bi, hp, i: (bi, 0, off + hp)),
                  pl.BlockSpec((1, t // tb, LANES, tb), lambda bi, hp, i: (bi, 0, off + hp, 0)),
                  pl.BlockSpec((1, nb, LANES), lambda bi, hp, i: (bi, 0, hp))],
        out_specs=pl.BlockSpec((1, tb, LANES), lambda bi, hp, i: (bi, i, hp)),
        out_shape=jax.ShapeDtypeStruct((b, t, MOBA_WIDTH), BF16),
        scratch_shapes=[pltpu.VMEM((HEADS_PER_LANE_BLOCK, -(-nb // SUBLANES) * SUBLANES, tb), F32)],
        compiler_params=_cparams("parallel", "parallel", "parallel"), name="moba_prompt",
    )(qb, kb, vt, kmean)


def _moba_sample_kernel(pt_ref, q_ref, kn_ref, vn_ref, *refs, n_new, n_seq):
    del pt_ref
    pages = [refs[4 * u:4 * u + 4] for u in range(n_seq)]
    o_ref, m_ref, l_ref, g_ref, acc_ref = refs[4 * n_seq:]
    s = pl.program_id(1)
    nbp = pl.num_programs(1) - 1
    rows, width = q_ref.shape[1:]
    page = pages[0][0].shape[2]
    blk_lane = lax.broadcasted_iota(jnp.int32, m_ref.shape[1:], 1)

    @pl.when(s == 0)
    def _():
        m_ref[...] = jnp.full(m_ref.shape, NEG_BIG, F32)
        l_ref[...] = jnp.zeros_like(l_ref)
        g_ref[...] = jnp.zeros_like(g_ref)

    @pl.when(s < nbp)
    def _():
        for u in range(n_seq):
            ka_ref, kb_ref, va_ref, vb_ref = pages[u]
            q = q_ref[u]
            flat = lambda ref: ref[...].reshape(width, page).astype(BF16)
            kt = jnp.concatenate([flat(ka_ref), flat(kb_ref)], axis=1)
            vt = jnp.concatenate([flat(va_ref), flat(vb_ref)], axis=1)
            sc = jnp.dot(q, kt, preferred_element_type=F32)
            m = jnp.max(sc, axis=1, keepdims=True)
            p = jnp.exp(sc - m)
            acc_ref[u, s] = lax.dot_general(p.astype(BF16), vt, _NT, preferred_element_type=F32)
            m_ref[u] = jnp.where(blk_lane == s, m, m_ref[u])
            l_ref[u] = jnp.where(blk_lane == s, jnp.sum(p, axis=1, keepdims=True), l_ref[u])
            g_ref[u] = jnp.where(blk_lane == s, jnp.sum(sc, axis=1, keepdims=True), g_ref[u])

    @pl.when(s == nbp)
    def _():
        for u in range(n_seq):
            q = q_ref[u]
            sel = _top_rows(g_ref[u], blk_lane < nbp, MOBA_TOPK, 1)
            s_own = jnp.dot(q, kn_ref[u].astype(BF16), preferred_element_type=F32)
            key = lax.broadcasted_iota(jnp.int32, s_own.shape, 1)
            tok = lax.broadcasted_iota(jnp.int32, s_own.shape, 0) % n_new
            s_own = jnp.where(key <= tok, s_own, -jnp.inf)
            m_all, l_all = m_ref[u], l_ref[u]
            m_tot = jnp.maximum(jnp.max(s_own, axis=1, keepdims=True),
                                jnp.max(jnp.where(sel, m_all, -jnp.inf), axis=1, keepdims=True))
            a = jnp.where(sel, jnp.exp(m_all - m_tot), 0.0)
            p_own = jnp.exp(s_own - m_tot)
            denom = jnp.sum(a * l_all, axis=1, keepdims=True) + jnp.sum(p_own, axis=1, keepdims=True)
            num = lax.dot_general(p_own.astype(BF16), vn_ref[u].astype(BF16), _NT, preferred_element_type=F32)
            for j in range(acc_ref.shape[1]):
                num = num + a[:, j:j + 1] * acc_ref[u, j]
            o_ref[u] = _fold_heads(num, n_new) / denom


def _moba_sample(layer, page_table, q_rows, kt_new, vt_new, cache_kt, cache_vt, n_new):
    bs, rows, width = q_rows.shape
    n_pages = page_table.shape[1]
    page = cache_kt.shape[4]
    new_lanes = kt_new.shape[2]
    per_blk = MOBA_BLOCK // page
    assert per_blk == 2 and n_pages % per_blk == 0
    nbp = n_pages // per_blk
    assert nbp <= LANES
    n_seq = next(c for c in (4, 2, 1) if bs % c == 0)
    head_blk = SB_HEADS // MOBA_HEADS

    def page_spec(u, which):
        return pl.BlockSpec(
            (None, None, MOBA_HEADS, HEAD_DIM, page),
            lambda b, s, pt: (layer, pt[b * n_seq + u, per_blk * jnp.minimum(s, nbp - 1) + which], head_blk, 0, 0))

    seq = lambda b, s, pt: (b, 0, 0)
    page_specs, page_args = [], []
    for u in range(n_seq):
        page_specs += [page_spec(u, 0), page_spec(u, 1), page_spec(u, 0), page_spec(u, 1)]
        page_args += [cache_kt, cache_kt, cache_vt, cache_vt]
    grid_spec = pltpu.PrefetchScalarGridSpec(
        num_scalar_prefetch=1, grid=(bs // n_seq, nbp + 1),
        in_specs=[pl.BlockSpec((n_seq, rows, width), seq),
                  pl.BlockSpec((n_seq, width, new_lanes), seq),
                  pl.BlockSpec((n_seq, width, new_lanes), seq)] + page_specs,
        out_specs=pl.BlockSpec((n_seq, rows, HEAD_DIM), seq),
        scratch_shapes=[pltpu.VMEM((n_seq, rows, LANES), F32), pltpu.VMEM((n_seq, rows, LANES), F32),
                        pltpu.VMEM((n_seq, rows, LANES), F32),
                        pltpu.VMEM((n_seq, nbp, rows, width), F32)])
    return pl.pallas_call(
        functools.partial(_moba_sample_kernel, n_new=n_new, n_seq=n_seq),
        grid_spec=grid_spec, out_shape=jax.ShapeDtypeStruct((bs, rows, HEAD_DIM), F32),
        compiler_params=_cparams("parallel", "arbitrary"), name="moba_sample",
    )(page_table, q_rows, kt_new, vt_new, *page_args)


def _layer_norm(t, g, b):
    mu = jnp.mean(t, axis=-1, keepdims=True)
    tc = t - mu
    var = jnp.mean(tc * tc, axis=-1, keepdims=True)
    return tc * lax.rsqrt(var + LN_EPS) * g + b


def _post_attn_kernel(osb_ref, omb_ref, g_ref, x_ref, wsb_ref, wmb_ref, wout_ref, lg_ref, lb_ref,
                      h_ref, hb_ref, hp_ref, *, alpha):
    d = x_ref.shape[1]
    bsb = jnp.dot(osb_ref[...], wsb_ref[...], preferred_element_type=F32)
    bmb = jnp.dot(omb_ref[...], wmb_ref[...], preferred_element_type=F32)
    merged = (jax.nn.sigmoid(g_ref[:, :d].astype(F32)) * bsb
              + jax.nn.sigmoid(g_ref[:, d:].astype(F32)) * bmb)
    t = alpha * x_ref[...] + jnp.dot(merged.astype(BF16), wout_ref[...], preferred_element_type=F32)
    h = _layer_norm(t, lg_ref[...], lb_ref[...])
    h_ref[...] = h
    hb_ref[...] = h.astype(BF16)
    hp_ref[...] = _pack_pairs(h)


def _post_attn(osb, omb, g, x2d, w, alpha):
    n, d = x2d.shape
    tm = min(ROW_TILE, n)
    row = lambda i: (i, 0)
    const = lambda i: (0, 0)
    return pl.pallas_call(
        functools.partial(_post_attn_kernel, alpha=alpha),
        grid=(n // tm,),
        in_specs=[pl.BlockSpec((tm, SB_WIDTH), row), pl.BlockSpec((tm, MOBA_WIDTH), row),
                  pl.BlockSpec((tm, 2 * d), row), pl.BlockSpec((tm, d), row),
                  pl.BlockSpec((SB_WIDTH, d), const), pl.BlockSpec((MOBA_WIDTH, d), const),
                  pl.BlockSpec((d, d), const), pl.BlockSpec((1, d), const), pl.BlockSpec((1, d), const)],
        out_specs=[pl.BlockSpec((tm, d), row), pl.BlockSpec((tm, d), row), pl.BlockSpec((tm, d // 2), row)],
        out_shape=[jax.ShapeDtypeStruct((n, d), F32), jax.ShapeDtypeStruct((n, d), BF16),
                   jax.ShapeDtypeStruct((n, d // 2), U32)],
        compiler_params=_cparams("parallel"), name="post_attn",
    )(osb, omb, g, x2d, w["wsb"], w["wmb"], w["wout"], w["ln1_g"], w["ln1_b"])


def _router_kernel(h_ref, wh_ref, wl_ref, b_ref, e_ref, g_ref):
    hh, hl = _split_bf16(h_ref[...])
    logits = (lax.dot_general(wh_ref[...], hh, _NT, preferred_element_type=F32)
              + lax.dot_general(wh_ref[...], hl, _NT, preferred_element_type=F32)
              + lax.dot_general(wl_ref[...], hh, _NT, preferred_element_type=F32))
    scores = jax.nn.sigmoid(logits)
    biased = scores + b_ref[...]
    n_exp, _ = scores.shape
    per_group = n_exp // N_EXPERT_GROUPS
    group_scores = []
    for g in range(N_EXPERT_GROUPS):
        v = biased[g * per_group:(g + 1) * per_group]
        top2 = _top_rows(v, jnp.ones(v.shape, jnp.bool_), 2, 0)
        group_scores.append(jnp.sum(jnp.where(top2, v, 0.0), axis=0, keepdims=True))
    group_scores = jnp.concatenate(group_scores, axis=0)
    keep = _top_rows(group_scores, jnp.ones(group_scores.shape, jnp.bool_), TOPK_GROUPS, 0)
    masked = jnp.concatenate(
        [jnp.where(keep[g:g + 1, :], biased[g * per_group:(g + 1) * per_group], -jnp.inf)
         for g in range(N_EXPERT_GROUPS)], axis=0)
    row = lax.broadcasted_iota(jnp.int32, masked.shape, 0)
    idxs, gates = [], []
    for _ in range(EXPERT_TOPK):
        m = jnp.max(masked, axis=0, keepdims=True)
        first = jnp.min(jnp.where(masked == m, row, n_exp), axis=0, keepdims=True)
        pick = row == first
        idxs.append(first)
        gates.append(jnp.sum(jnp.where(pick, scores, 0.0), axis=0, keepdims=True))
        masked = jnp.where(pick, -jnp.inf, masked)
    gate = jnp.concatenate(gates, axis=0)
    gate = gate / jnp.sum(gate, axis=0, keepdims=True) * ROUTED_SCALE
    e_ref[...] = jnp.concatenate(idxs, axis=0)
    g_ref[...] = gate


def _router(h, w):
    n, d = h.shape
    tm = min(ROW_TILE, n)
    n_exp = w["wr_hi"].shape[0]
    const = lambda i: (0, 0)
    return pl.pallas_call(
        _router_kernel, grid=(n // tm,),
        in_specs=[pl.BlockSpec((tm, d), lambda i: (i, 0)), pl.BlockSpec((n_exp, d), const),
                  pl.BlockSpec((n_exp, d), const), pl.BlockSpec((n_exp, 1), const)],
        out_specs=[pl.BlockSpec((EXPERT_TOPK, tm), lambda i: (0, i))] * 2,
        out_shape=[jax.ShapeDtypeStruct((EXPERT_TOPK, n), jnp.int32),
                   jax.ShapeDtypeStruct((EXPERT_TOPK, n), F32)],
        compiler_params=_cparams("parallel"), name="router",
    )(h, w["wr_hi"], w["wr_lo"], w["b_router"])


def _experts_kernel(wb_ref, we_ref, ws_ref, wt_ref, x_ref, gate_ref, wgu_ref, wdn_ref, y_ref):
    del we_ref
    w = pl.program_id(0)
    blk = wb_ref[w]
    start, end = ws_ref[w], wt_ref[w]
    rb, half = x_ref.shape

    @pl.when(jnp.logical_or(w == 0, wb_ref[jnp.maximum(w - 1, 0)] != blk))
    def _():
        y_ref[...] = jnp.zeros_like(y_ref)

    @pl.when(end > start)
    def _():
        de = wdn_ref.shape[1]
        x_lo, x_hi = _unpack_pairs(x_ref[...])
        hgu = (jnp.dot(x_lo.astype(BF16), wgu_ref[0, :half, :], preferred_element_type=F32)
               + jnp.dot(x_hi.astype(BF16), wgu_ref[0, half:, :], preferred_element_type=F32))
        act = jax.nn.silu(hgu[:, :de]) * hgu[:, de:]
        y = jnp.dot(act.astype(BF16), wdn_ref[0], preferred_element_type=F32) * gate_ref[...]
        r = blk * rb + lax.broadcasted_iota(jnp.int32, (rb, 1), 0)
        mine = jnp.logical_and(r >= start, r < end)
        y_ref[...] = jnp.where(mine, _pack_pairs(y), y_ref[...])


def _experts(work, xs, row_gate, w_gu, w_dn):
    rows, half = xs.shape
    rb = MOE_ROW_BLOCK
    d, de2 = w_gu.shape[1:]
    n_work = work[0].shape[0]
    grid_spec = pltpu.PrefetchScalarGridSpec(
        num_scalar_prefetch=4, grid=(n_work,),
        in_specs=[pl.BlockSpec((rb, half), lambda w, wb, we, ws, wt: (wb[w], 0)),
                  pl.BlockSpec((rb, 1), lambda w, wb, we, ws, wt: (wb[w], 0)),
                  pl.BlockSpec((1, d, de2), lambda w, wb, we, ws, wt: (we[w], 0, 0)),
                  pl.BlockSpec((1, de2 // 2, d), lambda w, wb, we, ws, wt: (we[w], 0, 0))],
        out_specs=pl.BlockSpec((rb, half), lambda w, wb, we, ws, wt: (wb[w], 0)))
    return pl.pallas_call(
        _experts_kernel, grid_spec=grid_spec, out_shape=jax.ShapeDtypeStruct((rows, half), U32),
        compiler_params=_cparams("arbitrary"), name="experts",
    )(*work, xs, row_gate, w_gu, w_dn)


def _final_kernel(h_ref, hb_ref, r_ref, p_ref, wsg_ref, wsd_ref, wpg_ref, wpp_ref, lg_ref, lb_ref,
                  y_ref, *, alpha):
    hb = hb_ref[...]
    de = wsd_ref.shape[0]
    sgu = jnp.dot(hb, wsg_ref[...], preferred_element_type=F32)
    shared = jnp.dot((jax.nn.silu(sgu[:, :de]) * sgu[:, de:]).astype(BF16), wsd_ref[...],
                     preferred_element_type=F32)
    ple = (jax.nn.sigmoid(jnp.dot(hb, wpg_ref[...], preferred_element_type=F32))
           * jnp.dot(p_ref[...].astype(BF16), wpp_ref[...], preferred_element_type=F32))
    parts = [_unpack_pairs(r_ref[k]) for k in range(r_ref.shape[0])]
    routed = jnp.concatenate([functools.reduce(jnp.add, [p[0] for p in parts]),
                              functools.reduce(jnp.add, [p[1] for p in parts])], axis=1)
    t = alpha * h_ref[...] + (routed + shared) + ple
    y_ref[...] = _layer_norm(t, lg_ref[...], lb_ref[...])


def _final(h, hb, routed_rows, routed_row0, p2d, w, alpha):
    n, d = h.shape
    tm = min(ROW_TILE, n)
    assert routed_row0 % tm == 0
    r_off = routed_row0 // tm
    k = routed_rows.shape[0]
    pd = p2d.shape[1]
    de2 = w["wsg"].shape[1]
    row = lambda i: (i, 0)
    const = lambda i: (0, 0)
    return pl.pallas_call(
        functools.partial(_final_kernel, alpha=alpha),
        grid=(n // tm,),
        in_specs=[pl.BlockSpec((tm, d), row), pl.BlockSpec((tm, d), row),
                  pl.BlockSpec((k, tm, d // 2), lambda i: (0, i + r_off, 0)), pl.BlockSpec((tm, pd), row),
                  pl.BlockSpec((d, de2), const), pl.BlockSpec((de2 // 2, d), const),
                  pl.BlockSpec((d, d), const), pl.BlockSpec((pd, d), const),
                  pl.BlockSpec((1, d), const), pl.BlockSpec((1, d), const)],
        out_specs=pl.BlockSpec((tm, d), row),
        out_shape=jax.ShapeDtypeStruct((n, d), F32),
        compiler_params=_cparams("parallel"), name="final",
    )(h, hb, routed_rows, p2d, w["wsg"], w["wsd"], w["wpg"], w["wpp"], w["ln2_g"], w["ln2_b"])


def _gather_rows(table, idx):
    width = table.shape[1]
    flat = idx.reshape(-1)
    n_idx = flat.shape[0]
    info = plsc.get_sparse_core_info()
    n_cores = info.num_cores
    n_workers = n_cores * info.num_subcores
    assert n_idx % (n_workers * SC_GATHER_ROWS) == 0
    per_worker = n_idx // n_workers
    mesh = plsc.VectorSubcoreMesh(core_axis_name="c", subcore_axis_name="s")

    @functools.partial(
        pl.kernel, mesh=mesh, out_type=jax.ShapeDtypeStruct((n_idx, width), table.dtype),
        scratch_types=[pltpu.VMEM((SC_GATHER_ROWS,), jnp.int32),
                       pltpu.VMEM((SC_GATHER_ROWS, width), table.dtype),
                       pltpu.SemaphoreType.DMA])
    def gather(table_hbm, idx_hbm, out_hbm, idx_v, rows_v, sem):
        base = (lax.axis_index("s") * n_cores + lax.axis_index("c")) * per_worker

        @pl.loop(0, per_worker // SC_GATHER_ROWS)
        def _(j):
            off = pl.multiple_of(base + j * SC_GATHER_ROWS, SC_GATHER_ROWS)
            pltpu.sync_copy(idx_hbm.at[pl.ds(off, SC_GATHER_ROWS)], idx_v)
            pltpu.async_copy(table_hbm.at[idx_v], rows_v, sem).wait()
            pltpu.sync_copy(rows_v, out_hbm.at[pl.ds(off, SC_GATHER_ROWS)])

    return gather(table, flat).reshape(*idx.shape, width)


def _moe_routed(hp, top_e_t, gate_t, w_gu, w_dn):
    n = hp.shape[0]
    n_exp = w_gu.shape[0]
    k = top_e_t.shape[0]
    rb = MOE_ROW_BLOCK
    n_assign = n * k
    n_blocks = -(-n_assign // rb)
    rows = n_blocks * rb
    ids = jnp.arange(n_assign, dtype=jnp.int32)
    flat_e = top_e_t.reshape(-1)
    _, sorted_id, sorted_gate = lax.sort((flat_e, ids, gate_t.reshape(-1)), num_keys=1, is_stable=True)
    _, pos = lax.sort((sorted_id, ids), num_keys=1)
    sorted_tok = jnp.pad(sorted_id % n, (0, rows - n_assign))
    sorted_gate = jnp.pad(sorted_gate, (0, rows - n_assign))

    counts = jnp.sum((flat_e[:, None] == jnp.arange(n_exp, dtype=jnp.int32)[None, :]).astype(jnp.int32), axis=0)
    group_end = jnp.cumsum(counts)
    starts = jnp.sort(jnp.concatenate([jnp.arange(n_blocks, dtype=jnp.int32) * rb,
                                       group_end[:n_exp - 1].astype(jnp.int32)]))
    ends = jnp.concatenate([starts[1:], jnp.full((1,), n_assign, jnp.int32)])
    work_blk = jnp.minimum(starts // rb, n_blocks - 1)
    work_exp = jnp.minimum(jnp.sum((group_end[None, :] <= starts[:, None]).astype(jnp.int32), axis=1), n_exp - 1)

    xs = _gather_rows(hp, sorted_tok)
    y = _experts((work_blk, work_exp, starts, ends), xs, sorted_gate[:, None], w_gu, w_dn)
    return _gather_rows(y, pos.reshape(k, n))


def _expand_heads(q2d, n_seq, n_new, n_heads):
    q = q2d.reshape(n_seq, n_new, n_heads, HEAD_DIM)
    eye = jnp.eye(n_heads, dtype=q2d.dtype)
    return jnp.einsum("bthd,hg->bhtgd", q, eye).reshape(n_seq, n_heads * n_new, n_heads * HEAD_DIM)


def _token_major(o, n_seq, n_new, n_heads):
    o = o.reshape(n_seq, n_heads, n_new, HEAD_DIM)
    return jnp.transpose(o, (0, 2, 1, 3)).reshape(n_seq * n_new, n_heads * HEAD_DIM)


def _new_keys_t(kv2d, n_seq, n_new, n_lanes):
    a = jnp.transpose(kv2d.reshape(n_seq, n_new, -1), (0, 2, 1))
    return jnp.pad(a, ((0, 0), (0, 0), (0, n_lanes - n_new)))


def _prep_weights(i, w_in, w_branch_sb, w_branch_moba, w_out, ln1_g, ln1_b, w_router, b_router,
                  w_sh_gu, w_sh_down, w_ple_gate, w_ple_proj, ln2_g, ln2_b):
    wi = w_in[i]
    d = wi.shape[0]
    s, m = SB_WIDTH, MOBA_WIDTH
    q_sb, k_sb, v_sb = wi[:, 0:s], wi[:, s:2 * s], wi[:, 2 * s:3 * s]
    o = 3 * s
    q_mb, k_mb, v_mb = wi[:, o:o + m], wi[:, o + m:o + 2 * m], wi[:, o + 2 * m:o + 3 * m]
    wv = jnp.concatenate([v_sb, v_mb], axis=1).astype(BF16)
    wr_hi, wr_lo = _split_bf16(w_router[i].T)
    return {
        "wq": jnp.concatenate([q_sb, q_mb], axis=1).astype(BF16),
        "wk": jnp.concatenate([k_sb, k_mb], axis=1).astype(BF16),
        "wv": wv, "wvt": wv.T, "wg": wi[:, o + 3 * m:].astype(BF16),
        "wsb": w_branch_sb[i].astype(BF16), "wmb": w_branch_moba[i].astype(BF16),
        "wout": w_out[i].astype(BF16),
        "ln1_g": ln1_g[i].reshape(1, d), "ln1_b": ln1_b[i].reshape(1, d),
        "wr_hi": wr_hi, "wr_lo": wr_lo, "b_router": b_router[i].reshape(-1, 1).astype(F32),
        "wsg": w_sh_gu[i].astype(BF16), "wsd": w_sh_down[i].astype(BF16),
        "wpg": w_ple_gate[i].astype(BF16), "wpp": w_ple_proj[i].astype(BF16),
        "ln2_g": ln2_g[i].reshape(1, d), "ln2_b": ln2_b[i].reshape(1, d),
    }


def kernel(x_prompt, x_sample, cache_k, cache_v, page_table, p_prompt, p_sample, w_in, w_branch_sb, w_branch_moba, w_out, ln1_g, ln1_b, w_router, b_router, w_exp_gu, w_exp_down, w_sh_gu, w_sh_down, w_ple_gate, w_ple_proj, ln2_g, ln2_b):
    depth = w_in.shape[0]
    alpha = (2 * depth) ** 0.25
    bp, tp, d = x_prompt.shape
    bs, ts, _ = x_sample.shape
    n_p, n_s = bp * tp, bs * ts
    n_pages = page_table.shape[1]
    page = cache_k.shape[2]
    past = n_pages * page
    width = SB_WIDTH + MOBA_WIDTH
    n_heads = SB_HEADS + MOBA_HEADS
    assert SB_HEADS == MOBA_HEADS and cache_k.shape[3] == n_heads
    assert tp % ATT_BLOCK == 0 and past % MOBA_BLOCK == 0 and ts <= LANES
    assert n_p % min(ROW_TILE, n_s) == 0

    tab_p = _rope_tables(jnp.arange(tp))
    tab_s = _rope_tables(past + (jnp.arange(n_s) % ts))
    cache_kt = jnp.transpose(cache_k, (0, 1, 3, 4, 2))
    cache_vt = jnp.transpose(cache_v, (0, 1, 3, 4, 2))

    h_p = x_prompt.reshape(n_p, d)
    h_s = x_sample.reshape(n_s, d)
    k_p_rows, v_p_rows, k_s_rows, v_s_rows = [], [], [], []
    for i in range(depth):
        w = _prep_weights(i, w_in, w_branch_sb, w_branch_moba, w_out, ln1_g, ln1_b, w_router, b_router,
                          w_sh_gu, w_sh_down, w_ple_gate, w_ple_proj, ln2_g, ln2_b)

        kf, vf, qb, kb, g_p, vt, km = _inproj(h_p, w, tab_p, tp, True)
        k_p_rows.append(kf.reshape(bp, tp, n_heads, HEAD_DIM))
        v_p_rows.append(vf.reshape(bp, tp, n_heads, HEAD_DIM))
        qb3, kb3 = qb.reshape(bp, tp, width), kb.reshape(bp, tp, width)
        osb_p = _sb_prompt(qb3, kb3, vt).reshape(n_p, SB_WIDTH)
        omb_p = _moba_prompt(qb3, kb3, vt, km.reshape(bp, tp // MOBA_BLOCK, MOBA_WIDTH)).reshape(n_p, MOBA_WIDTH)
        h1_p, h1b_p, h1p_p = _post_attn(osb_p, omb_p, g_p, h_p, w, alpha)

        kf_s, vf_s, qb_s, g_s = _inproj(h_s, w, tab_s, ts, False)
        k_s_rows.append(kf_s.reshape(bs, ts, n_heads, HEAD_DIM))
        v_s_rows.append(vf_s.reshape(bs, ts, n_heads, HEAD_DIM))
        o_sb = _sb_sample(i, page_table, _expand_heads(qb_s[:, :SB_WIDTH], bs, ts, SB_HEADS),
                          _new_keys_t(kf_s[:, :SB_WIDTH], bs, ts, LANES),
                          _new_keys_t(vf_s[:, :SB_WIDTH], bs, ts, LANES), cache_kt, cache_vt, ts)
        o_mb = _moba_sample(i, page_table, _expand_heads(qb_s[:, SB_WIDTH:], bs, ts, MOBA_HEADS),
                            _new_keys_t(kf_s[:, SB_WIDTH:], bs, ts, LANES),
                            _new_keys_t(vf_s[:, SB_WIDTH:], bs, ts, LANES), cache_kt, cache_vt, ts)
        osb_s = _token_major(o_sb, bs, ts, SB_HEADS).astype(BF16)
        omb_s = _token_major(o_mb, bs, ts, MOBA_HEADS).astype(BF16)
        h1_s, h1b_s, h1p_s = _post_attn(osb_s, omb_s, g_s, h_s, w, alpha)

        e_p, g_rp = _router(h1_p, w)
        e_s, g_rs = _router(h1_s, w)
        routed_rows = _moe_routed(jnp.concatenate([h1p_p, h1p_s], axis=0),
                                  jnp.concatenate([e_p, e_s], axis=1), jnp.concatenate([g_rp, g_rs], axis=1),
                                  w_exp_gu[i].astype(BF16), w_exp_down[i].astype(BF16))

        h_p = _final(h1_p, h1b_p, routed_rows, 0, p_prompt[i].reshape(n_p, -1), w, alpha)
        h_s = _final(h1_s, h1b_s, routed_rows, n_p, p_sample[i].reshape(n_s, -1), w, alpha)

    return (h_p.reshape(bp, tp, d), h_s.reshape(bs, ts, d),
            jnp.stack(k_p_rows), jnp.stack(v_p_rows), jnp.stack(k_s_rows), jnp.stack(v_s_rows))
```
